```python
import jax
import jax.numpy as jnp
from jax import lax
import numpy as np

D_MODEL = 1024
BATCH = 4
SEQ = 8192
DEPTH = 1
DEC_BATCH = 16
DEC_SEQ = 16
PAST_LEN = 4096

CHUNK = 64
D_PLE = 256
R_HEADS = 8
R_HEAD_DIM = 64
D_R = R_HEADS * R_HEAD_DIM
LORA_W = 64
LORA_A = 64
LORA_G = 128
R_PROJ = 3 * D_R + LORA_W + LORA_A + LORA_G
G_HEADS = 4
G_HEAD_DIM = 128
D_G = G_HEADS * G_HEAD_DIM
CONV_W = 4
G_QKV = 3 * D_G
D_MIX = D_R + D_G
OFF_QKV = R_PROJ
OFF_Z = OFF_QKV + G_QKV
OFF_B = OFF_Z + D_G
OFF_A = OFF_B + G_HEADS
D_IN = OFF_A + G_HEADS
D_FF = -(-(8 * D_MODEL) // (3 * 256)) * 256
RMS_EPS = 1e-6
GN_EPS = 64e-5

kernel_name = 'hybrid_rwkv7_gdn_stream_step'


def _rmsnorm(x, g):
    x32 = x.astype(jnp.float32)
    y = x32 * lax.rsqrt(jnp.mean(x32 * x32, axis=-1, keepdims=True) + RMS_EPS)
    return (y * g.astype(jnp.float32)).astype(x.dtype)


def _l2norm(x):
    return x * lax.rsqrt(jnp.sum(x * x, axis=-1, keepdims=True) + 1e-6)


def _rwkv7_scan(r, w, k, v, kk, a, S0):
    def step(S, xs):
        r_t, w_t, k_t, v_t, kk_t, a_t = xs
        sa = jnp.einsum('bhvk,bhk->bhv', S, kk_t)
        S = (S * w_t[:, :, None, :]
             - jnp.einsum('bhv,bhk->bhvk', sa, kk_t * a_t)
             + jnp.einsum('bhv,bhk->bhvk', v_t, k_t))
        y = jnp.einsum('bhvk,bhk->bhv', S, r_t)
        return S, y
    xs = tuple(jnp.moveaxis(t, 1, 0) for t in (r, w, k, v, kk, a))
    S, y = lax.scan(step, S0, xs)
    return jnp.moveaxis(y, 0, 1), S


def _rwkv7_group(f, shift_prev, S0, mu, w0, w_up, a0, a_up, g_up, k_k, k_a, r_k, lnx_w, lnx_b):
    B, L, _ = f.shape
    f32 = jnp.float32
    f_prev = jnp.concatenate([shift_prev[:, None, :].astype(f.dtype), f[:, :-1]], axis=1)
    fm = f + (f_prev - f) * mu
    r, k, v, wl, al, gl = jnp.split(
        fm, [D_R, 2 * D_R, 3 * D_R, 3 * D_R + LORA_W, 3 * D_R + LORA_W + LORA_A], axis=-1)
    w_log = -jax.nn.softplus(-(w0 + jnp.tanh(wl) @ w_up).astype(f32)) - 0.5
    decay = jnp.exp(-jnp.exp(w_log))
    a = jax.nn.sigmoid((a0 + al @ a_up).astype(f32))
    g = (jax.nn.sigmoid(gl) @ g_up).astype(f32)
    hd = lambda t: t.reshape(B, L, R_HEADS, R_HEAD_DIM)
    k32 = k.astype(f32)
    kk = hd(k32 * k_k)
    kk = kk * lax.rsqrt(jnp.sum(kk * kk, axis=-1, keepdims=True) + 1e-12)
    k_eff = hd(k32 * (1.0 + (a - 1.0) * k_a))
    rh = hd(r.astype(f32))
    vh = hd(v.astype(f32))
    y, S = _rwkv7_scan(rh, hd(decay), k_eff, vh, kk, hd(a), S0.astype(f32))
    mean = jnp.mean(y, axis=-1, keepdims=True)
    var = jnp.mean(jnp.square(y - mean), axis=-1, keepdims=True)
    yn = ((y - mean) * lax.rsqrt(var + GN_EPS)).reshape(B, L, D_R) * lnx_w + lnx_b
    bonus = (jnp.sum(rh * k_eff * r_k, axis=-1, keepdims=True) * vh).reshape(B, L, D_R)
    out = ((yn + bonus) * g).astype(f.dtype)
    return out, f[:, -1], S


def _gated_delta_chunked(q, k, v, glog, beta, S0):
    B, L, H, Dk = q.shape
    Dv = v.shape[-1]
    C = min(CHUNK, L)
    N = L // C
    blk = lambda t: jnp.moveaxis(t.reshape((B, N, C) + t.shape[2:]), 3, 1)
    q, k, v, glog, beta = blk(q), blk(k), blk(v), blk(glog), blk(beta)
    G = jnp.cumsum(glog, axis=-1)
    idx = jnp.arange(C)
    incl = idx[:, None] >= idx[None, :]
    strict = idx[:, None] > idx[None, :]
    dG = G[..., :, None] - G[..., None, :]
    dec_incl = jnp.where(incl, jnp.exp(jnp.where(incl, dG, 0.0)), 0.0)
    dec_strict = jnp.where(strict, dec_incl, 0.0)
    kb = k * beta[..., None]
    A = jnp.einsum('bhnid,bhnjd->bhnij', kb, k) * dec_strict
    eye = jnp.eye(C, dtype=jnp.float32)
    T = lax.linalg.triangular_solve(eye + A, jnp.broadcast_to(eye, A.shape),
                                    left_side=True, lower=True)
    U = jnp.einsum('bhnij,bhnje->bhnie', T, v * beta[..., None])
    Wk = jnp.einsum('bhnij,bhnjd->bhnid', T, kb * jnp.exp(G)[..., None])
    qk = jnp.einsum('bhnid,bhnjd->bhnij', q, k) * dec_incl
    qg = q * jnp.exp(G)[..., None]
    G_last = G[..., -1]
    kd = k * jnp.exp(G_last[..., None] - G)[..., None]

    def step(S, xs):
        U_n, W_n, qg_n, qk_n, kd_n, gl_n = xs
        v_new = U_n - jnp.einsum('bhcd,bhde->bhce', W_n, S)
        o = jnp.einsum('bhcd,bhde->bhce', qg_n, S) + jnp.einsum('bhij,bhje->bhie', qk_n, v_new)
        S = S * jnp.exp(gl_n)[..., None, None] + jnp.einsum('bhcd,bhce->bhde', kd_n, v_new)
        return S, o

    xs = tuple(jnp.moveaxis(t, 2, 0) for t in (U, Wk, qg, qk, kd, G_last))
    S, o = lax.scan(step, S0, xs)
    o = jnp.transpose(o, (1, 0, 3, 2, 4)).reshape(B, L, H, Dv)
    return o, S


def _gdn_group(qkv, z, b, a_raw, conv_prev, S0, conv_w, a_log, dt_bias, norm_g):
    B, L, _ = qkv.shape
    f32 = jnp.float32
    full = jnp.concatenate([conv_prev.astype(qkv.dtype), qkv], axis=1)
    conv = full[:, 0:L] * conv_w[0]
    for j in range(1, CONV_W):
        conv = conv + full[:, j:j + L] * conv_w[j]
    conv = jax.nn.silu(conv.astype(f32))
    q, k, v = jnp.split(conv, [D_G, 2 * D_G], axis=-1)
    hd = lambda t: t.reshape(B, L, G_HEADS, G_HEAD_DIM)
    q = _l2norm(hd(q)) * (G_HEAD_DIM ** -0.5)
    k = _l2norm(hd(k))
    v = hd(v)
    beta = jax.nn.sigmoid(b.astype(f32))
    glog = -jnp.exp(a_log.astype(f32)) * jax.nn.softplus(a_raw.astype(f32) + dt_bias)
    o, S = _gated_delta_chunked(q, k, v, glog, beta, S0.astype(f32))
    o = (o * lax.rsqrt(jnp.mean(o * o, axis=-1, keepdims=True) + RMS_EPS) * norm_g
         * jax.nn.silu(hd(z.astype(f32))))
    return o.reshape(B, L, D_G).astype(qkv.dtype), full[:, L:], S


def _layer(x, p, shift0, wkv0, conv0, gdn0, ln_mix_g, w_in, mu_shift, w0, w_lora_up, a0,
           a_lora_up, g_lora_up, k_k, k_a, r_k, ln_x_w, ln_x_b, conv_w, a_log, dt_bias,
           gdn_norm_g, w_out, ln_ffn_g, w_gate, w_up, w_down, ln_ple_g, w_ple_gate, w_ple_proj):
    h = _rmsnorm(x, ln_mix_g)
    proj = h @ w_in
    r_out, shift1, wkv1 = _rwkv7_group(proj[..., :R_PROJ], shift0, wkv0, mu_shift, w0, w_lora_up,
                                       a0, a_lora_up, g_lora_up, k_k, k_a, r_k, ln_x_w, ln_x_b)
    g_out, conv1, gdn1 = _gdn_group(proj[..., OFF_QKV:OFF_Z], proj[..., OFF_Z:OFF_B],
                                    proj[..., OFF_B:OFF_A], proj[..., OFF_A:], conv0, gdn0,
                                    conv_w, a_log, dt_bias, gdn_norm_g)
    x = x + jnp.concatenate([r_out, g_out], axis=-1) @ w_out
    h = _rmsnorm(x, ln_ffn_g)
    x = x + (jax.nn.silu(h @ w_gate) * (h @ w_up)) @ w_down
    h = _rmsnorm(x, ln_ple_g)
    x = x + jax.nn.sigmoid(h @ w_ple_gate) * (p @ w_ple_proj)
    return (x, shift1.astype(shift0.dtype), wkv1.astype(wkv0.dtype),
            conv1.astype(conv0.dtype), gdn1.astype(gdn0.dtype))


def setup_inputs(seed: int = 0) -> dict:
    key = jax.random.key(seed)
    ks = iter(jax.random.split(key, 48))
    f32 = jnp.float32
    nrm = lambda shape, s: jax.random.normal(next(ks), shape, f32) * s
    uni = lambda shape, lo, hi: jax.random.uniform(next(ks), shape, f32, lo, hi)
    dt = jnp.exp(uni((DEPTH, G_HEADS), float(np.log(1e-3)), float(np.log(1e-1))))
    return {
        'x_prompt': nrm((BATCH, SEQ, D_MODEL), 1.0),
        'x_sample': nrm((DEC_BATCH, DEC_SEQ, D_MODEL), 1.0),
        'state_shift': nrm((DEPTH, DEC_BATCH, R_PROJ), 1.0),
        'state_wkv': nrm((DEPTH, DEC_BATCH, R_HEADS, R_HEAD_DIM, R_HEAD_DIM), 0.1),
        'state_conv': nrm((DEPTH, DEC_BATCH, CONV_W - 1, G_QKV), 1.0),
        'state_gdn': nrm((DEPTH, DEC_BATCH, G_HEADS, G_HEAD_DIM, G_HEAD_DIM), 0.1),
        'p_prompt': nrm((DEPTH, BATCH, SEQ, D_PLE), 1.0),
        'p_sample': nrm((DEPTH, DEC_BATCH, DEC_SEQ, D_PLE), 1.0),
        'ln_mix_g': 1.0 + nrm((DEPTH, D_MODEL), 0.02),
        'w_in': nrm((DEPTH, D_MODEL, D_IN), D_MODEL ** -0.5),
        'mu_shift': uni((DEPTH, R_PROJ), 0.0, 1.0),
        'w0': uni((DEPTH, D_R), -5.0, 0.0),
        'w_lora_up': nrm((DEPTH, LORA_W, D_R), 0.1),
        'a0': nrm((DEPTH, D_R), 0.1),
        'a_lora_up': nrm((DEPTH, LORA_A, D_R), 0.1),
        'g_lora_up': nrm((DEPTH, LORA_G, D_R), LORA_G ** -0.5),
        'k_k': 0.85 + nrm((DEPTH, D_R), 0.05),
        'k_a': 1.0 + nrm((DEPTH, D_R), 0.05),
        'r_k': nrm((DEPTH, R_HEADS, R_HEAD_DIM), 0.1),
        'ln_x_w': 1.0 + nrm((DEPTH, D_R), 0.02),
        'ln_x_b': nrm((DEPTH, D_R), 0.02),
        'conv_w': nrm((DEPTH, CONV_W, G_QKV), 0.5),
        'a_log': jnp.log(uni((DEPTH, G_HEADS), 1.0, 16.0)),
        'dt_bias': dt + jnp.log(-jnp.expm1(-dt)),
        'gdn_norm_g': 1.0 + nrm((DEPTH, G_HEAD_DIM), 0.02),
        'w_out': nrm((DEPTH, D_MIX, D_MODEL), D_MIX ** -0.5),
        'ln_ffn_g': 1.0 + nrm((DEPTH, D_MODEL), 0.02),
        'w_gate': nrm((DEPTH, D_MODEL, D_FF), D_MODEL ** -0.5),
        'w_up': nrm((DEPTH, D_MODEL, D_FF), D_MODEL ** -0.5),
        'w_down': nrm((DEPTH, D_FF, D_MODEL), D_FF ** -0.5),
        'ln_ple_g': 1.0 + nrm((DEPTH, D_MODEL), 0.02),
        'w_ple_gate': nrm((DEPTH, D_MODEL, D_MODEL), D_MODEL ** -0.5),
        'w_ple_proj': nrm((DEPTH, D_PLE, D_MODEL), D_PLE ** -0.5),
        'final_norm_g': 1.0 + nrm((D_MODEL,), 0.02),
    }


def reference(x_prompt, x_sample, state_shift, state_wkv, state_conv, state_gdn, p_prompt, p_sample,
              ln_mix_g, w_in, mu_shift, w0, w_lora_up, a0, a_lora_up, g_lora_up, k_k, k_a, r_k,
              ln_x_w, ln_x_b, conv_w, a_log, dt_bias, gdn_norm_g, w_out, ln_ffn_g, w_gate, w_up,
              w_down, ln_ple_g, w_ple_gate, w_ple_proj, final_norm_g):
    Bp = x_prompt.shape[0]
    dt = x_prompt.dtype
    z_shift = jnp.zeros((Bp, R_PROJ), dt)
    z_wkv = jnp.zeros((Bp, R_HEADS, R_HEAD_DIM, R_HEAD_DIM), dt)
    z_conv = jnp.zeros((Bp, CONV_W - 1, G_QKV), dt)
    z_gdn = jnp.zeros((Bp, G_HEADS, G_HEAD_DIM, G_HEAD_DIM), dt)
    yp, ys = x_prompt, x_sample
    sp_shift, sp_wkv, sp_conv, sp_gdn = [], [], [], []
    ss_shift, ss_wkv, ss_conv, ss_gdn = [], [], [], []
    for l in range(DEPTH):
        wts = (ln_mix_g[l], w_in[l], mu_shift[l], w0[l], w_lora_up[l], a0[l], a_lora_up[l],
               g_lora_up[l], k_k[l], k_a[l], r_k[l], ln_x_w[l], ln_x_b[l], conv_w[l], a_log[l],
               dt_bias[l], gdn_norm_g[l], w_out[l], ln_ffn_g[l], w_gate[l], w_up[l], w_down[l],
               ln_ple_g[l], w_ple_gate[l], w_ple_proj[l])
        yp, a1, a2, a3, a4 = _layer(yp, p_prompt[l], z_shift, z_wkv, z_conv, z_gdn, *wts)
        ys, b1, b2, b3, b4 = _layer(ys, p_sample[l], state_shift[l], state_wkv[l],
                                    state_conv[l], state_gdn[l], *wts)
        sp_shift.append(a1); sp_wkv.append(a2); sp_conv.append(a3); sp_gdn.append(a4)
        ss_shift.append(b1); ss_wkv.append(b2); ss_conv.append(b3); ss_gdn.append(b4)
    y_prompt = _rmsnorm(yp, final_norm_g)
    y_sample = _rmsnorm(ys, final_norm_g)
    new_shift_p = jnp.stack(sp_shift)
    new_wkv_p = jnp.stack(sp_wkv)
    new_conv_p = jnp.stack(sp_conv)
    new_gdn_p = jnp.stack(sp_gdn)
    new_shift_s = jnp.stack(ss_shift)
    new_wkv_s = jnp.stack(ss_wkv)
    new_conv_s = jnp.stack(ss_conv)
    new_gdn_s = jnp.stack(ss_gdn)
    return (y_prompt, y_sample, new_shift_p, new_wkv_p, new_conv_p, new_gdn_p,
            new_shift_s, new_wkv_s, new_conv_s, new_gdn_s)
```

```python
import functools

import jax
import jax.numpy as jnp
from jax import lax
from jax.experimental import pallas as pl
from jax.experimental.pallas import tpu as pltpu

F32 = jnp.float32
BF16 = jnp.bfloat16

R_HEADS = 8
R_HEAD_DIM = 64
D_R = R_HEADS * R_HEAD_DIM
LORA_W = 64
LORA_A = 64
LORA_G = 128
R_PROJ = 3 * D_R + LORA_W + LORA_A + LORA_G
G_HEADS = 4
G_HEAD_DIM = 128
D_G = G_HEADS * G_HEAD_DIM
CONV_W = 4
G_QKV = 3 * D_G
CHUNK = 64
RMS_EPS = 1e-6
GN_EPS = 64e-5

LANES = 128
SUBLANES = 8
VMEM_LIMIT_BYTES = 56 * 1024 * 1024
INV_BLOCK = 16
FF_TILE = 256


def _dot(a, b, nt=False):
    dn = (((1,), (1,)), ((), ())) if nt else (((1,), (0,)), ((), ()))
    return lax.dot_general(a, b, dn, preferred_element_type=F32)


def _split(a):
    hi = a.astype(BF16)
    lo = (a - hi.astype(F32)).astype(BF16)
    return hi, lo


def _mm1(a, b, nt=False):
    return _dot(a.astype(BF16), b.astype(BF16), nt)


def _mm3(a, b, nt=False):
    ah, al = _split(a)
    bh, bl = _split(b)
    return _dot(ah, bh, nt) + (_dot(ah, bl, nt) + _dot(al, bh, nt))


def _mm_exact_lhs(a_bf16, b):
    b1 = b.astype(BF16)
    r1 = b - b1.astype(F32)
    b2 = r1.astype(BF16)
    b3 = (r1 - b2.astype(F32)).astype(BF16)
    return _dot(a_bf16, b1) + (_dot(a_bf16, b2) + _dot(a_bf16, b3))


def _mm_exact_rhs(a, b_bf16):
    a1 = a.astype(BF16)
    r1 = a - a1.astype(F32)
    a2 = r1.astype(BF16)
    a3 = (r1 - a2.astype(F32)).astype(BF16)
    return _dot(a1, b_bf16) + (_dot(a2, b_bf16) + _dot(a3, b_bf16))


def _transpose(x, eye):
    x1 = x.astype(BF16)
    r1 = x - x1.astype(F32)
    x2 = r1.astype(BF16)
    x3 = (r1 - x2.astype(F32)).astype(BF16)
    return _dot(eye, x1, nt=True) + (_dot(eye, x2, nt=True) + _dot(eye, x3, nt=True))


def _sigmoid(x):
    return 1.0 / (1.0 + jnp.exp(-x))


def _softplus(x):
    return jnp.maximum(x, 0.0) + jnp.log1p(jnp.exp(-jnp.abs(x)))


def _tri_masks(t):
    row = lax.broadcasted_iota(jnp.int32, (t, t), 0)
    col = lax.broadcasted_iota(jnp.int32, (t, t), 1)
    return row, col


def _tri_inv(a, mm):
    t = a.shape[0]
    assert t <= 4 * INV_BLOCK
    row, col = _tri_masks(t)
    eye = (row == col).astype(F32)
    if t > INV_BLOCK:
        blk = (row // INV_BLOCK) == (col // INV_BLOCK)
        d = jnp.where(blk, a, 0.0)
    else:
        d = a
    d2 = mm(d, d)
    x = eye - d
    x = x + mm(x, d2)
    d4 = mm(d2, d2)
    x = x + mm(x, d4)
    d8 = mm(d4, d4)
    x = x + mm(x, d8)
    if t <= INV_BLOCK:
        return x
    m = mm(x, a - d)
    m2 = mm(m, m)
    n = eye - m
    n = n + mm(n, m2)
    return mm(n, x)


def _resident(shape):
    nd = len(shape)
    return pl.BlockSpec(shape, lambda *_: (0,) * nd)


def _inproj_kernel(x_ref, g_ref, wr_ref, wq_ref, wz_ref, wb_ref, f_ref, q_ref, z_ref, b_ref):
    x = x_ref[...]
    h = x * lax.rsqrt(jnp.mean(x * x, axis=-1, keepdims=True) + RMS_EPS) * g_ref[...]
    hb = h.astype(BF16)
    f_ref[...] = _dot(hb, wr_ref[...])
    q_ref[...] = _dot(hb, wq_ref[...])
    z_ref[...] = _dot(hb, wz_ref[...])
    b_ref[...] = _dot(hb, wb_ref[...])


def _inproj(x2, g, wr, wq, wz, wb, tm):
    n, d = x2.shape
    outs = (R_PROJ, G_QKV, D_G, LANES)
    return pl.pallas_call(
        _inproj_kernel,
        grid=(n // tm,),
        in_specs=[pl.BlockSpec((tm, d), lambda i: (i, 0)), _resident(g.shape),
                  _resident(wr.shape), _resident(wq.shape), _resident(wz.shape), _resident(wb.shape)],
        out_specs=[pl.BlockSpec((tm, w), lambda i: (i, 0)) for w in outs],
        out_shape=[jax.ShapeDtypeStruct((n, w), F32) for w in outs],
        compiler_params=pltpu.CompilerParams(dimension_semantics=("arbitrary",),
                                             vmem_limit_bytes=VMEM_LIMIT_BYTES),
        name="inproj",
    )(x2, g, wr, wq, wz, wb)


def _rwkv_kernel(t, f_ref, shift_ref, s0_ref, mu_ref, w0_ref, wlo_ref, alo_ref, a0_ref, gup_ref,
                 kk_ref, ka_ref, rk_ref, lnw_ref, lnb_ref, seg_ref, eye_ref,
                 out_ref, sout_ref, fbuf, state, ybuf):
    c_idx = pl.program_id(1)
    base = SUBLANES

    @pl.when(c_idx == 0)
    def _():
        fbuf[base - 1:base, :] = shift_ref[0]
        state[...] = s0_ref[0]

    f = f_ref[0]
    fbuf[base:base + t, :] = f
    fp = fbuf[base - 1:base - 1 + t, :]
    fbuf[base - 1:base, :] = f[t - 1:t, :]
    fm = f + (fp - f) * mu_ref[...]

    r = fm[:, 0:D_R]
    k = fm[:, D_R:2 * D_R]
    v = fm[:, 2 * D_R:3 * D_R]
    wa = fm[:, 3 * D_R:3 * D_R + LORA_W + LORA_A]
    gl = fm[:, 3 * D_R + LORA_W + LORA_A:]

    seg = seg_ref[...]

    def segsum(x):
        xh, xl = _split(x)
        return _dot(xh, seg) + _dot(xl, seg)

    w_log = -_softplus(-(w0_ref[...] + _mm1(jnp.tanh(wa), wlo_ref[...]))) - 0.5
    ld = -jnp.exp(w_log)
    a = _sigmoid(a0_ref[...] + _mm1(wa, alo_ref[...]))
    g = _mm1(_sigmoid(gl), gup_ref[...])
    kk = k * kk_ref[...]
    kk = kk * lax.rsqrt(segsum(kk * kk) + 1e-12)
    keff = k * (1.0 + (a - 1.0) * ka_ref[...])
    kka = kk * a

    row, col = _tri_masks(t)
    incl = row >= col
    strict = row > col
    cum = _mm_exact_lhs(incl.astype(BF16), ld)
    cl = cum[t - 1:t, :]
    e_mc = jnp.exp(-cum)
    rt = r * jnp.exp(cum)
    kpt = kk * jnp.exp(cum - ld)
    bt = kka * e_mc
    ktl = keff * e_mc

    eye = eye_ref[...]
    cum_t = _transpose(cum, eye)
    clt = cum_t[:, t - 1:t]
    e_lct = jnp.exp(clt - cum_t)
    khat_t = _transpose(keff, eye) * e_lct
    bhat_t = _transpose(kka, eye) * e_lct
    pt_col = jnp.exp(clt)

    mm = _mm3
    for h in range(R_HEADS):
        sl = slice(h * R_HEAD_DIM, (h + 1) * R_HEAD_DIM)
        kpt_h = kpt[:, sl]
        rt_h = rt[:, sl]
        v_h = v[:, sl]
        lhs = jnp.concatenate([kpt_h, rt_h], axis=0)
        gb = mm(lhs, bt[:, sl], nt=True)
        gk = mm(lhs, ktl[:, sl], nt=True)
        a_kb = jnp.where(strict, gb[:t], 0.0)
        a_rb = jnp.where(incl, gb[t:], 0.0)
        a_kk = jnp.where(strict, gk[:t], 0.0)
        a_rk = jnp.where(incl, gk[t:], 0.0)
        tinv = _tri_inv(a_kb, mm)
        av = mm(jnp.concatenate([a_kk, a_rk], axis=0), v_h)
        w_k = mm(tinv, kpt_h)
        u = mm(tinv, av[:t])
        s = state[h]
        ws = mm(jnp.concatenate([w_k, rt_h], axis=0), s)
        z = ws[:t] + u
        y = ws[t:] + av[t:] - mm(a_rb, z)
        s_new = s * pt_col[sl] + mm(khat_t[sl], v_h) - mm(bhat_t[sl], z)
        state[h] = s_new
        ybuf[:, sl] = y

    y = ybuf[...]
    inv_n = 1.0 / R_HEAD_DIM
    mean = segsum(y) * inv_n
    dlt = y - mean
    var = segsum(dlt * dlt) * inv_n
    yn = dlt * lax.rsqrt(var + GN_EPS) * lnw_ref[...] + lnb_ref[...]
    bonus = segsum(r * keff * rk_ref[...]) * v
    out_ref[0] = ((yn + bonus) * g).astype(out_ref.dtype)

    @pl.when(c_idx == pl.num_programs(1) - 1)
    def _():
        sout_ref[0] = state[...]


def _rwkv(f3, shift0, s0_kv, mu, w0, wlo, alo, a0, gup, k_k, k_a, r_k, lnw, lnb, seg, eye):
    b, l, _ = f3.shape
    t = min(CHUNK, l)
    nc = l // t
    row = lambda a: _resident(a.shape)
    return pl.pallas_call(
        functools.partial(_rwkv_kernel, t),
        grid=(b, nc),
        in_specs=[pl.BlockSpec((1, t, R_PROJ), lambda i, c: (i, c, 0)),
                  pl.BlockSpec((1, 1, R_PROJ), lambda i, c: (i, 0, 0)),
                  pl.BlockSpec((1, R_HEADS, R_HEAD_DIM, R_HEAD_DIM), lambda i, c: (i, 0, 0, 0)),
                  row(mu), row(w0), row(wlo), row(alo), row(a0), row(gup),
                  row(k_k), row(k_a), row(r_k), row(lnw), row(lnb), row(seg), row(eye)],
        out_specs=[pl.BlockSpec((1, t, D_R), lambda i, c: (i, c, 0)),
                   pl.BlockSpec((1, R_HEADS, R_HEAD_DIM, R_HEAD_DIM), lambda i, c: (i, 0, 0, 0))],
        out_shape=[jax.ShapeDtypeStruct((b, l, D_R), BF16),
                   jax.ShapeDtypeStruct((b, R_HEADS, R_HEAD_DIM, R_HEAD_DIM), F32)],
        scratch_shapes=[pltpu.VMEM((SUBLANES + t, R_PROJ), F32),
                        pltpu.VMEM((R_HEADS, R_HEAD_DIM, R_HEAD_DIM), F32),
                        pltpu.VMEM((t, D_R), F32)],
        compiler_params=pltpu.CompilerParams(dimension_semantics=("arbitrary", "arbitrary"),
                                             vmem_limit_bytes=VMEM_LIMIT_BYTES),
        name="rwkv7",
    )(f3, shift0, s0_kv, mu, w0, wlo, alo, a0, gup, k_k, k_a, r_k, lnw, lnb, seg, eye)


def _gdn_kernel(t, qkv_ref, z_ref, ba_ref, conv0_ref, s0_ref, cw_ref, alog_ref, dtb_ref, ng_ref,
                eye_ref, out_ref, sout_ref, cbuf, state):
    c_idx = pl.program_id(1)
    base = SUBLANES
    hist = CONV_W - 1

    @pl.when(c_idx == 0)
    def _():
        cbuf[base - hist:base, :] = conv0_ref[0]
        state[...] = s0_ref[0]

    qkv = qkv_ref[0]
    cbuf[base:base + t, :] = qkv
    conv = qkv * cw_ref[hist:hist + 1, :]
    for j in range(hist):
        conv = conv + cbuf[base - hist + j:base - hist + j + t, :] * cw_ref[j:j + 1, :]
    cbuf[base - hist:base, :] = qkv[t - hist:t, :]
    conv = conv * _sigmoid(conv)

    ba = ba_ref[0]
    beta_t = _sigmoid(ba)
    glog_t = -jnp.exp(alog_ref[...]) * _softplus(ba + dtb_ref[...])
    row, col = _tri_masks(t)
    incl = row >= col
    strict = row > col
    g_t = _mm_exact_lhs(incl.astype(BF16), glog_t)
    g_tt = _transpose(g_t, eye_ref[0:LANES, 0:LANES])

    k_all = conv[:, D_G:2 * D_G]
    k_norm = []
    for h in range(G_HEADS):
        kh = k_all[:, h * G_HEAD_DIM:(h + 1) * G_HEAD_DIM]
        k_norm.append(kh * lax.rsqrt(jnp.sum(kh * kh, axis=-1, keepdims=True) + 1e-6))
    kt_all = _transpose(jnp.concatenate(k_norm, axis=1), eye_ref[...])

    mm = _mm3
    for h in range(G_HEADS):
        sl = slice(h * G_HEAD_DIM, (h + 1) * G_HEAD_DIM)
        qh = conv[:, sl]
        qh = qh * lax.rsqrt(jnp.sum(qh * qh, axis=-1, keepdims=True) + 1e-6) * (G_HEAD_DIM ** -0.5)
        kh = k_norm[h]
        vh = conv[:, 2 * D_G + h * G_HEAD_DIM:2 * D_G + (h + 1) * G_HEAD_DIM]
        bcol = beta_t[:, h:h + 1]
        gcol = g_t[:, G_HEADS + h:G_HEADS + h + 1]
        grow = g_tt[G_HEADS + h:G_HEADS + h + 1, :]
        dg = gcol - grow
        dec_incl = jnp.where(incl, jnp.exp(jnp.where(incl, dg, 0.0)), 0.0)
        dec_strict = jnp.where(strict, dec_incl, 0.0)
        kb = kh * bcol
        a_mat = mm(kb, kh, nt=True) * dec_strict
        tinv = _tri_inv(a_mat, mm)
        eg = jnp.exp(gcol)
        uw = mm(tinv, jnp.concatenate([vh * bcol, kb * eg], axis=1))
        u = uw[:, :G_HEAD_DIM]
        w = uw[:, G_HEAD_DIM:]
        qk = mm(qh, kh, nt=True) * dec_incl
        qg = qh * eg
        glast = gcol[t - 1:t, :]
        kd_t = kt_all[sl] * jnp.exp(glast - grow)
        s = state[h]
        wqs = mm(jnp.concatenate([w, qg], axis=0), s)
        v_new = u - wqs[:t]
        o = wqs[t:] + mm(qk, v_new)
        state[h] = s * jnp.exp(glast) + mm(kd_t, v_new)
        zh = z_ref[0, :, sl]
        o = (o * lax.rsqrt(jnp.mean(o * o, axis=-1, keepdims=True) + RMS_EPS) * ng_ref[...]
             * (zh * _sigmoid(zh)))
        out_ref[0, :, sl] = o.astype(out_ref.dtype)

    @pl.when(c_idx == pl.num_programs(1) - 1)
    def _():
        sout_ref[0] = state[...]


def _gdn(qkv3, z3, ba3, conv0, s0, cw, alog, dtb, ng, eye):
    b, l, _ = qkv3.shape
    t = min(CHUNK, l)
    nc = l // t
    row = lambda a: _resident(a.shape)
    return pl.pallas_call(
        functools.partial(_gdn_kernel, t),
        grid=(b, nc),
        in_specs=[pl.BlockSpec((1, t, G_QKV), lambda i, c: (i, c, 0)),
                  pl.BlockSpec((1, t, D_G), lambda i, c: (i, c, 0)),
                  pl.BlockSpec((1, t, LANES), lambda i, c: (i, c, 0)),
                  pl.BlockSpec((1, CONV_W - 1, G_QKV), lambda i, c: (i, 0, 0)),
                  pl.BlockSpec((1, G_HEADS, G_HEAD_DIM, G_HEAD_DIM), lambda i, c: (i, 0, 0, 0)),
                  row(cw), row(alog), row(dtb), row(ng), row(eye)],
        out_specs=[pl.BlockSpec((1, t, D_G), lambda i, c: (i, c, 0)),
                   pl.BlockSpec((1, G_HEADS, G_HEAD_DIM, G_HEAD_DIM), lambda i, c: (i, 0, 0, 0))],
        out_shape=[jax.ShapeDtypeStruct((b, l, D_G), BF16),
                   jax.ShapeDtypeStruct((b, G_HEADS, G_HEAD_DIM, G_HEAD_DIM), F32)],
        scratch_shapes=[pltpu.VMEM((SUBLANES + t, G_QKV), F32),
                        pltpu.VMEM((G_HEADS, G_HEAD_DIM, G_HEAD_DIM), F32)],
        compiler_params=pltpu.CompilerParams(dimension_semantics=("arbitrary", "arbitrary"),
                                             vmem_limit_bytes=VMEM_LIMIT_BYTES),
        name="gdn",
    )(qkv3, z3, ba3, conv0, s0, cw, alog, dtb, ng, eye)


def _rms(x, g):
    return x * lax.rsqrt(jnp.mean(x * x, axis=-1, keepdims=True) + RMS_EPS) * g


def _tail_kernel(x_ref, r_ref, g_ref, p_ref, wor_ref, wog_ref, lnf_ref, wgate_ref, wup_ref, wdown_ref,
                 lnp_ref, wpg_ref, wpp_ref, fng_ref, y_ref):
    x = x_ref[...] + _dot(r_ref[...], wor_ref[...]) + _dot(g_ref[...], wog_ref[...])
    hb = _rms(x, lnf_ref[...]).astype(BF16)
    d_ff = wgate_ref.shape[1]
    ffn = None
    for j in range(d_ff // FF_TILE):
        cs = slice(j * FF_TILE, (j + 1) * FF_TILE)
        gate = _dot(hb, wgate_ref[:, cs])
        up = _dot(hb, wup_ref[:, cs])
        act = (gate * _sigmoid(gate)) * up
        down = _dot(act.astype(BF16), wdown_ref[cs, :])
        ffn = down if ffn is None else ffn + down
    x = x + ffn
    hb = _rms(x, lnp_ref[...]).astype(BF16)
    x = x + _sigmoid(_dot(hb, wpg_ref[...])) * _dot(p_ref[...].astype(BF16), wpp_ref[...])
    y_ref[...] = _rms(x, fng_ref[...])


def _tail(x2, r2, g2, p2, wor, wog, lnf, wgate, wup, wdown, lnp, wpg, wpp, fng, tm):
    n, d = x2.shape
    tok = lambda w: pl.BlockSpec((tm, w), lambda i: (i, 0))
    consts = (wor, wog, lnf, wgate, wup, wdown, lnp, wpg, wpp, fng)
    return pl.pallas_call(
        _tail_kernel,
        grid=(n // tm,),
        in_specs=[tok(d), tok(r2.shape[1]), tok(g2.shape[1]), tok(p2.shape[1])]
                 + [_resident(a.shape) for a in consts],
        out_specs=tok(d),
        out_shape=jax.ShapeDtypeStruct((n, d), F32),
        compiler_params=pltpu.CompilerParams(dimension_semantics=("arbitrary",),
                                             vmem_limit_bytes=VMEM_LIMIT_BYTES),
        name="tail",
    )(x2, r2, g2, p2, *consts)


def _pad_rows(a, rows):
    return jnp.concatenate([a, jnp.zeros((rows - a.shape[0],) + a.shape[1:], a.dtype)], axis=0)


def _pad_lanes(a, lanes, offset=0):
    out = jnp.zeros(a.shape[:-1] + (lanes,), a.dtype)
    return lax.dynamic_update_slice_in_dim(out, a, offset, axis=a.ndim - 1)


def _layer(x, p, shift0, wkv0, conv0, gdn0, wts, final_g):
    (ln_mix_g, w_in, mu_shift, w0, w_lora_up, a0, a_lora_up, g_lora_up, k_k, k_a, r_k, ln_x_w, ln_x_b,
     conv_w, a_log, dt_bias, gdn_norm_g, w_out, ln_ffn_g, w_gate, w_up, w_down, ln_ple_g,
     w_ple_gate, w_ple_proj) = wts
    b, l, d = x.shape
    n = b * l
    tm = min(256, n)
    row = lambda a: a.reshape(1, -1).astype(F32)

    off_z = R_PROJ + G_QKV
    off_b = off_z + D_G
    w_in_b = w_in.astype(BF16)
    wr, wq, wz = w_in_b[:, :R_PROJ], w_in_b[:, R_PROJ:off_z], w_in_b[:, off_z:off_b]
    wb = _pad_lanes(w_in_b[:, off_b:], LANES)
    x2 = x.reshape(n, d)
    f2, qkv2, z2, ba2 = _inproj(x2, row(ln_mix_g), wr, wq, wz, wb, tm)

    zeros_lora = jnp.zeros((LORA_W, D_R), BF16)
    wlo = jnp.concatenate([w_lora_up.astype(BF16), zeros_lora], axis=0)
    alo = jnp.concatenate([zeros_lora, a_lora_up.astype(BF16)], axis=0)
    head_id = jnp.arange(D_R) // R_HEAD_DIM
    seg = (head_id[:, None] == head_id[None, :]).astype(BF16)
    eye = jnp.eye(D_R, dtype=BF16)
    f3 = f2.reshape(b, l, R_PROJ)
    r_out, wkv1_kv = _rwkv(f3, shift0.reshape(b, 1, R_PROJ), jnp.swapaxes(wkv0, -1, -2),
                           row(mu_shift), row(w0), wlo, alo, row(a0), g_lora_up.astype(BF16),
                           row(k_k), row(k_a), row(r_k), row(ln_x_w), row(ln_x_b), seg, eye)
    wkv1 = jnp.swapaxes(wkv1_kv, -1, -2)
    shift1 = f3[:, -1]

    qkv3 = qkv2.reshape(b, l, G_QKV)
    g_out, gdn1 = _gdn(qkv3, z2.reshape(b, l, D_G), ba2.reshape(b, l, LANES), conv0, gdn0,
                       conv_w, _pad_lanes(row(a_log), LANES, G_HEADS),
                       _pad_lanes(row(dt_bias), LANES, G_HEADS), row(gdn_norm_g), eye)
    conv1 = jnp.concatenate([conv0, qkv3], axis=1)[:, l:]

    w_out_b = w_out.astype(BF16)
    y2 = _tail(x2, r_out.reshape(n, D_R), g_out.reshape(n, D_G), p.reshape(n, -1),
               w_out_b[:D_R], w_out_b[D_R:], row(ln_ffn_g), w_gate.astype(BF16), w_up.astype(BF16),
               w_down.astype(BF16), row(ln_ple_g), w_ple_gate.astype(BF16), w_ple_proj.astype(BF16),
               row(final_g), tm)
    return y2.reshape(b, l, d), shift1, wkv1, conv1, gdn1


def kernel(x_prompt, x_sample, state_shift, state_wkv, state_conv, state_gdn, p_prompt, p_sample,
           ln_mix_g, w_in, mu_shift, w0, w_lora_up, a0, a_lora_up, g_lora_up, k_k, k_a, r_k,
           ln_x_w, ln_x_b, conv_w, a_log, dt_bias, gdn_norm_g, w_out, ln_ffn_g, w_gate, w_up,
           w_down, ln_ple_g, w_ple_gate, w_ple_proj, final_norm_g):
    depth = w_in.shape[0]
    assert depth == 1, "the final norm is fused into the single layer's tail kernel"
    bp = x_prompt.shape[0]
    dt = x_prompt.dtype
    wts = tuple(a[0] for a in (ln_mix_g, w_in, mu_shift, w0, w_lora_up, a0, a_lora_up, g_lora_up, k_k,
                               k_a, r_k, ln_x_w, ln_x_b, conv_w, a_log, dt_bias, gdn_norm_g, w_out,
                               ln_ffn_g, w_gate, w_up, w_down, ln_ple_g, w_ple_gate, w_ple_proj))
    z_shift = jnp.zeros((bp, R_PROJ), dt)
    z_wkv = jnp.zeros((bp, R_HEADS, R_HEAD_DIM, R_HEAD_DIM), dt)
    z_conv = jnp.zeros((bp, CONV_W - 1, G_QKV), dt)
    z_gdn = jnp.zeros((bp, G_HEADS, G_HEAD_DIM, G_HEAD_DIM), dt)
    yp, a1, a2, a3, a4 = _layer(x_prompt, p_prompt[0], z_shift, z_wkv, z_conv, z_gdn, wts, final_norm_g)
    ys, b1, b2, b3, b4 = _layer(x_sample, p_sample[0], state_shift[0], state_wkv[0], state_conv[0],
                                state_gdn[0], wts, final_norm_g)
    stack = lambda a: a[None]
    return (yp, ys, stack(a1), stack(a2), stack(a3), stack(a4),
            stack(b1), stack(b2), stack(b3), stack(b4))
```

```python
import functools

import jax
import jax.numpy as jnp
from jax import lax
from jax.experimental import pallas as pl
from jax.experimental.pallas import tpu as pltpu

F32 = jnp.float32
BF16 = jnp.bfloat16

R_HEADS = 8
R_HEAD_DIM = 64
D_R = R_HEADS * R_HEAD_DIM
LORA_W = 64
LORA_A = 64
LORA_G = 128
R_PROJ = 3 * D_R + LORA_W + LORA_A + LORA_G
G_HEADS = 4
G_HEAD_DIM = 128
D_G = G_HEADS * G_HEAD_DIM
CONV_W = 4
G_QKV = 3 * D_G
CHUNK = 64
RMS_EPS = 1e-6
GN_EPS = 64e-5

LANES = 128
SUBLANES = 8
VMEM_LIMIT_BYTES = 56 * 1024 * 1024
INV_BLOCK = 16
FF_TILE = 256
GDN_BATCH_BLOCK = 2


def _dot(a, b, nt=False):
    dn = (((1,), (1,)), ((), ())) if nt else (((1,), (0,)), ((), ()))
    return lax.dot_general(a, b, dn, preferred_element_type=F32)


def _split(a):
    hi = a.astype(BF16)
    lo = (a - hi.astype(F32)).astype(BF16)
    return hi, lo


def _mm1(a, b, nt=False):
    return _dot(a.astype(BF16), b.astype(BF16), nt)


def _mm3(a, b, nt=False):
    ah, al = _split(a)
    bh, bl = _split(b)
    return _dot(ah, bh, nt) + (_dot(ah, bl, nt) + _dot(al, bh, nt))


def _mm_exact_lhs(a_bf16, b):
    b1 = b.astype(BF16)
    r1 = b - b1.astype(F32)
    b2 = r1.astype(BF16)
    b3 = (r1 - b2.astype(F32)).astype(BF16)
    return _dot(a_bf16, b1) + (_dot(a_bf16, b2) + _dot(a_bf16, b3))


def _mm_exact_rhs(a, b_bf16):
    a1 = a.astype(BF16)
    r1 = a - a1.astype(F32)
    a2 = r1.astype(BF16)
    a3 = (r1 - a2.astype(F32)).astype(BF16)
    return _dot(a1, b_bf16) + (_dot(a2, b_bf16) + _dot(a3, b_bf16))


def _transpose(x, eye):
    x1 = x.astype(BF16)
    r1 = x - x1.astype(F32)
    x2 = r1.astype(BF16)
    x3 = (r1 - x2.astype(F32)).astype(BF16)
    return _dot(eye, x1, nt=True) + (_dot(eye, x2, nt=True) + _dot(eye, x3, nt=True))


def _sigmoid(x):
    return 1.0 / (1.0 + jnp.exp(-x))


def _softplus(x):
    return jnp.maximum(x, 0.0) + jnp.log1p(jnp.exp(-jnp.abs(x)))


def _tri_masks(t):
    row = lax.broadcasted_iota(jnp.int32, (t, t), 0)
    col = lax.broadcasted_iota(jnp.int32, (t, t), 1)
    return row, col


def _tri_inv(a_list, mm):
    t = a_list[0].shape[0]
    assert t <= 4 * INV_BLOCK
    row, col = _tri_masks(t)
    eye = (row == col).astype(F32)
    if t > INV_BLOCK:
        blk = (row // INV_BLOCK) == (col // INV_BLOCK)
        d = [jnp.where(blk, a, 0.0) for a in a_list]
    else:
        d = a_list
    d2 = [mm(di, di) for di in d]
    x = [eye - di for di in d]
    d4 = [mm(p, p) for p in d2]
    x = [xi + mm(xi, p) for xi, p in zip(x, d2)]
    d8 = [mm(p, p) for p in d4]
    x = [xi + mm(xi, p) for xi, p in zip(x, d4)]
    x = [xi + mm(xi, p) for xi, p in zip(x, d8)]
    if t <= INV_BLOCK:
        return x
    m = [mm(xi, a - di) for xi, a, di in zip(x, a_list, d)]
    m2 = [mm(mi, mi) for mi in m]
    n = [eye - mi for mi in m]
    n = [ni + mm(ni, p) for ni, p in zip(n, m2)]
    return [mm(ni, xi) for ni, xi in zip(n, x)]


def _resident(shape):
    nd = len(shape)
    return pl.BlockSpec(shape, lambda *_: (0,) * nd)


def _inproj_kernel(x_ref, g_ref, wr_ref, wq_ref, wz_ref, wb_ref, f_ref, q_ref, z_ref, b_ref):
    x = x_ref[...]
    h = x * lax.rsqrt(jnp.mean(x * x, axis=-1, keepdims=True) + RMS_EPS) * g_ref[...]
    hb = h.astype(BF16)
    f_ref[...] = _dot(hb, wr_ref[...])
    q_ref[...] = _dot(hb, wq_ref[...])
    z_ref[...] = _dot(hb, wz_ref[...])
    b_ref[...] = _dot(hb, wb_ref[...])


def _inproj(x2, g, wr, wq, wz, wb, tm):
    n, d = x2.shape
    outs = (R_PROJ, G_QKV, D_G, LANES)
    return pl.pallas_call(
        _inproj_kernel,
        grid=(n // tm,),
        in_specs=[pl.BlockSpec((tm, d), lambda i: (i, 0)), _resident(g.shape),
                  _resident(wr.shape), _resident(wq.shape), _resident(wz.shape), _resident(wb.shape)],
        out_specs=[pl.BlockSpec((tm, w), lambda i: (i, 0)) for w in outs],
        out_shape=[jax.ShapeDtypeStruct((n, w), F32) for w in outs],
        compiler_params=pltpu.CompilerParams(dimension_semantics=("arbitrary",),
                                             vmem_limit_bytes=VMEM_LIMIT_BYTES),
        name="inproj",
    )(x2, g, wr, wq, wz, wb)


def _rwkv_kernel(t, f_ref, shift_ref, s0_ref, mu_ref, w0_ref, wlo_ref, alo_ref, a0_ref, gup_ref,
                 kk_ref, ka_ref, rk_ref, lnw_ref, lnb_ref, seg_ref, eye_ref,
                 out_ref, sout_ref, fbuf, state, ybuf):
    c_idx = pl.program_id(1)
    base = SUBLANES

    @pl.when(c_idx == 0)
    def _():
        fbuf[base - 1:base, :] = shift_ref[0]
        state[...] = s0_ref[0]

    f = f_ref[0]
    fbuf[base:base + t, :] = f
    fp = fbuf[base - 1:base - 1 + t, :]
    fbuf[base - 1:base, :] = f[t - 1:t, :]
    fm = f + (fp - f) * mu_ref[...]

    r = fm[:, 0:D_R]
    k = fm[:, D_R:2 * D_R]
    v = fm[:, 2 * D_R:3 * D_R]
    wa = fm[:, 3 * D_R:3 * D_R + LORA_W + LORA_A]
    gl = fm[:, 3 * D_R + LORA_W + LORA_A:]

    seg = seg_ref[...]

    def segsum(x):
        xh, xl = _split(x)
        return _dot(xh, seg) + _dot(xl, seg)

    w_log = -_softplus(-(w0_ref[...] + _mm1(jnp.tanh(wa), wlo_ref[...]))) - 0.5
    ld = -jnp.exp(w_log)
    a = _sigmoid(a0_ref[...] + _mm1(wa, alo_ref[...]))
    g = _mm1(_sigmoid(gl), gup_ref[...])
    kk = k * kk_ref[...]
    kk = kk * lax.rsqrt(segsum(kk * kk) + 1e-12)
    keff = k * (1.0 + (a - 1.0) * ka_ref[...])
    kka = kk * a

    row, col = _tri_masks(t)
    incl = row >= col
    strict = row > col
    cum = _mm_exact_lhs(incl.astype(BF16), ld)
    cl = cum[t - 1:t, :]
    e_mc = jnp.exp(-cum)
    rt = r * jnp.exp(cum)
    kpt = kk * jnp.exp(cum - ld)
    bt = kka * e_mc
    ktl = keff * e_mc

    eye = eye_ref[...]
    cum_t = _transpose(cum, eye)
    clt = cum_t[:, t - 1:t]
    e_lct = jnp.exp(clt - cum_t)
    khat_t = _transpose(keff, eye) * e_lct
    bhat_t = _transpose(kka, eye) * e_lct
    pt_col = jnp.exp(clt)

    mm = _mm1
    heads = range(R_HEADS)
    sls = [slice(h * R_HEAD_DIM, (h + 1) * R_HEAD_DIM) for h in heads]
    kpt_h = [kpt[:, sl] for sl in sls]
    rt_h = [rt[:, sl] for sl in sls]
    v_h = [v[:, sl] for sl in sls]
    lhs = [jnp.concatenate([kpt_h[h], rt_h[h]], axis=0) for h in heads]
    gb = [mm(lhs[h], bt[:, sls[h]], nt=True) for h in heads]
    gk = [mm(lhs[h], ktl[:, sls[h]], nt=True) for h in heads]
    a_kb = [jnp.where(strict, gb[h][:t], 0.0) for h in heads]
    a_rb = [jnp.where(incl, gb[h][t:], 0.0) for h in heads]
    a_kr = [jnp.concatenate([jnp.where(strict, gk[h][:t], 0.0), jnp.where(incl, gk[h][t:], 0.0)], axis=0)
            for h in heads]
    av = [mm(a_kr[h], v_h[h]) for h in heads]
    kv = [mm(khat_t[sls[h]], v_h[h]) for h in heads]
    tinv = _tri_inv(a_kb, mm)
    w_k = [mm(tinv[h], kpt_h[h]) for h in heads]
    u = [mm(tinv[h], av[h][:t]) for h in heads]
    s = [state[h] for h in heads]
    ws = [mm(jnp.concatenate([w_k[h], rt_h[h]], axis=0), s[h]) for h in heads]
    z = [ws[h][:t] + u[h] for h in heads]
    rbz = [mm(a_rb[h], z[h]) for h in heads]
    bz = [mm(bhat_t[sls[h]], z[h]) for h in heads]
    for h in heads:
        state[h] = s[h] * pt_col[sls[h]] + kv[h] - bz[h]
        ybuf[:, sls[h]] = ws[h][t:] + av[h][t:] - rbz[h]

    y = ybuf[...]
    inv_n = 1.0 / R_HEAD_DIM
    mean = segsum(y) * inv_n
    dlt = y - mean
    var = segsum(dlt * dlt) * inv_n
    yn = dlt * lax.rsqrt(var + GN_EPS) * lnw_ref[...] + lnb_ref[...]
    bonus = segsum(r * keff * rk_ref[...]) * v
    out_ref[0] = ((yn + bonus) * g).astype(out_ref.dtype)

    @pl.when(c_idx == pl.num_programs(1) - 1)
    def _():
        sout_ref[0] = state[...]


def _rwkv(f3, shift0, s0_kv, mu, w0, wlo, alo, a0, gup, k_k, k_a, r_k, lnw, lnb, seg, eye):
    b, l, _ = f3.shape
    t = min(CHUNK, l)
    nc = l // t
    row = lambda a: _resident(a.shape)
    return pl.pallas_call(
        functools.partial(_rwkv_kernel, t),
        grid=(b, nc),
        in_specs=[pl.BlockSpec((1, t, R_PROJ), lambda i, c: (i, c, 0)),
                  pl.BlockSpec((1, 1, R_PROJ), lambda i, c: (i, 0, 0)),
                  pl.BlockSpec((1, R_HEADS, R_HEAD_DIM, R_HEAD_DIM), lambda i, c: (i, 0, 0, 0)),
                  row(mu), row(w0), row(wlo), row(alo), row(a0), row(gup),
                  row(k_k), row(k_a), row(r_k), row(lnw), row(lnb), row(seg), row(eye)],
        out_specs=[pl.BlockSpec((1, t, D_R), lambda i, c: (i, c, 0)),
                   pl.BlockSpec((1, R_HEADS, R_HEAD_DIM, R_HEAD_DIM), lambda i, c: (i, 0, 0, 0))],
        out_shape=[jax.ShapeDtypeStruct((b, l, D_R), BF16),
                   jax.ShapeDtypeStruct((b, R_HEADS, R_HEAD_DIM, R_HEAD_DIM), F32)],
        scratch_shapes=[pltpu.VMEM((SUBLANES + t, R_PROJ), F32),
                        pltpu.VMEM((R_HEADS, R_HEAD_DIM, R_HEAD_DIM), F32),
                        pltpu.VMEM((t, D_R), F32)],
        compiler_params=pltpu.CompilerParams(dimension_semantics=("arbitrary", "arbitrary"),
                                             vmem_limit_bytes=VMEM_LIMIT_BYTES),
        name="rwkv7",
    )(f3, shift0, s0_kv, mu, w0, wlo, alo, a0, gup, k_k, k_a, r_k, lnw, lnb, seg, eye)


def _gdn_kernel(t, qkv_ref, z_ref, ba_ref, conv0_ref, s0_ref, cw_ref, alog_ref, dtb_ref, ng_ref,
                eye_ref, out_ref, sout_ref, cbuf, state):
    c_idx = pl.program_id(1)
    base = SUBLANES
    hist = CONV_W - 1

    bb = qkv_ref.shape[0]

    @pl.when(c_idx == 0)
    def _():
        cbuf[:, base - hist:base, :] = conv0_ref[...]
        state[...] = s0_ref[...]

    row, col = _tri_masks(t)
    incl = row >= col
    strict = row > col
    tri = incl.astype(BF16)
    mm = _mm1

    units = [(bi, h) for bi in range(bb) for h in range(G_HEADS)]
    qh, kh, vh, bcol, gcol, grow, glast, kt = {}, {}, {}, {}, {}, {}, {}, {}
    for bi in range(bb):
        qkv = qkv_ref[bi]
        cbuf[bi, base:base + t, :] = qkv
        conv = qkv * cw_ref[hist:hist + 1, :]
        for j in range(hist):
            conv = conv + cbuf[bi, base - hist + j:base - hist + j + t, :] * cw_ref[j:j + 1, :]
        cbuf[bi, base - hist:base, :] = qkv[t - hist:t, :]
        conv = conv * _sigmoid(conv)

        ba = ba_ref[bi]
        beta_t = _sigmoid(ba)
        glog_t = -jnp.exp(alog_ref[...]) * _softplus(ba + dtb_ref[...])
        g_t = _mm_exact_lhs(tri, glog_t)
        g_tt = _transpose(g_t, eye_ref[0:LANES, 0:LANES])
        k_norm = []
        for h in range(G_HEADS):
            sl = slice(h * G_HEAD_DIM, (h + 1) * G_HEAD_DIM)
            q = conv[:, sl]
            qh[bi, h] = q * lax.rsqrt(jnp.sum(q * q, axis=-1, keepdims=True) + 1e-6) * (G_HEAD_DIM ** -0.5)
            k = conv[:, D_G + h * G_HEAD_DIM:D_G + (h + 1) * G_HEAD_DIM]
            kh[bi, h] = k * lax.rsqrt(jnp.sum(k * k, axis=-1, keepdims=True) + 1e-6)
            k_norm.append(kh[bi, h])
            vh[bi, h] = conv[:, 2 * D_G + h * G_HEAD_DIM:2 * D_G + (h + 1) * G_HEAD_DIM]
            bcol[bi, h] = beta_t[:, h:h + 1]
            gcol[bi, h] = g_t[:, G_HEADS + h:G_HEADS + h + 1]
            grow[bi, h] = g_tt[G_HEADS + h:G_HEADS + h + 1, :]
            glast[bi, h] = gcol[bi, h][t - 1:t, :]
        kt_all = _transpose(jnp.concatenate(k_norm, axis=1), eye_ref[...])
        for h in range(G_HEADS):
            kt[bi, h] = kt_all[h * G_HEAD_DIM:(h + 1) * G_HEAD_DIM]

    dec_incl, kb, eg = {}, {}, {}
    for u_ in units:
        dg = gcol[u_] - grow[u_]
        dec_incl[u_] = jnp.where(incl, jnp.exp(jnp.where(incl, dg, 0.0)), 0.0)
        kb[u_] = kh[u_] * bcol[u_]
        eg[u_] = jnp.exp(gcol[u_])
    a_mat = [mm(kb[u_], kh[u_], nt=True) * jnp.where(strict, dec_incl[u_], 0.0) for u_ in units]
    qk = [mm(qh[u_], kh[u_], nt=True) * dec_incl[u_] for u_ in units]
    tinv = _tri_inv(a_mat, mm)
    uw = [mm(tinv[i], jnp.concatenate([vh[u_] * bcol[u_], kb[u_] * eg[u_]], axis=1))
          for i, u_ in enumerate(units)]
    s = [state[u_[0], u_[1]] for u_ in units]
    wqs = [mm(jnp.concatenate([uw[i][:, G_HEAD_DIM:], qh[u_] * eg[u_]], axis=0), s[i])
           for i, u_ in enumerate(units)]
    v_new = [uw[i][:, :G_HEAD_DIM] - wqs[i][:t] for i in range(len(units))]
    qkv_new = [mm(qk[i], v_new[i]) for i in range(len(units))]
    kdv = [mm(kt[u_] * jnp.exp(glast[u_] - grow[u_]), v_new[i]) for i, u_ in enumerate(units)]
    for i, (bi, h) in enumerate(units):
        sl = slice(h * G_HEAD_DIM, (h + 1) * G_HEAD_DIM)
        state[bi, h] = s[i] * jnp.exp(glast[bi, h]) + kdv[i]
        o = wqs[i][t:] + qkv_new[i]
        zh = z_ref[bi, :, sl]
        o = (o * lax.rsqrt(jnp.mean(o * o, axis=-1, keepdims=True) + RMS_EPS) * ng_ref[...]
             * (zh * _sigmoid(zh)))
        out_ref[bi, :, sl] = o.astype(out_ref.dtype)

    @pl.when(c_idx == pl.num_programs(1) - 1)
    def _():
        sout_ref[...] = state[...]


def _gdn(qkv3, z3, ba3, conv0, s0, cw, alog, dtb, ng, eye):
    b, l, _ = qkv3.shape
    t = min(CHUNK, l)
    nc = l // t
    bb = GDN_BATCH_BLOCK if b % GDN_BATCH_BLOCK == 0 else 1
    row = lambda a: _resident(a.shape)
    return pl.pallas_call(
        functools.partial(_gdn_kernel, t),
        grid=(b // bb, nc),
        in_specs=[pl.BlockSpec((bb, t, G_QKV), lambda i, c: (i, c, 0)),
                  pl.BlockSpec((bb, t, D_G), lambda i, c: (i, c, 0)),
                  pl.BlockSpec((bb, t, LANES), lambda i, c: (i, c, 0)),
                  pl.BlockSpec((bb, CONV_W - 1, G_QKV), lambda i, c: (i, 0, 0)),
                  pl.BlockSpec((bb, G_HEADS, G_HEAD_DIM, G_HEAD_DIM), lambda i, c: (i, 0, 0, 0)),
                  row(cw), row(alog), row(dtb), row(ng), row(eye)],
        out_specs=[pl.BlockSpec((bb, t, D_G), lambda i, c: (i, c, 0)),
                   pl.BlockSpec((bb, G_HEADS, G_HEAD_DIM, G_HEAD_DIM), lambda i, c: (i, 0, 0, 0))],
        out_shape=[jax.ShapeDtypeStruct((b, l, D_G), BF16),
                   jax.ShapeDtypeStruct((b, G_HEADS, G_HEAD_DIM, G_HEAD_DIM), F32)],
        scratch_shapes=[pltpu.VMEM((bb, SUBLANES + t, G_QKV), F32),
                        pltpu.VMEM((bb, G_HEADS, G_HEAD_DIM, G_HEAD_DIM), F32)],
        compiler_params=pltpu.CompilerParams(dimension_semantics=("arbitrary", "arbitrary"),
                                             vmem_limit_bytes=VMEM_LIMIT_BYTES),
        name="gdn",
    )(qkv3, z3, ba3, conv0, s0, cw, alog, dtb, ng, eye)


def _rms(x, g):
    return x * lax.rsqrt(jnp.mean(x * x, axis=-1, keepdims=True) + RMS_EPS) * g


def _tail_kernel(x_ref, r_ref, g_ref, p_ref, wor_ref, wog_ref, lnf_ref, wgate_ref, wup_ref, wdown_ref,
                 lnp_ref, wpg_ref, wpp_ref, fng_ref, y_ref):
    x = x_ref[...] + _dot(r_ref[...], wor_ref[...]) + _dot(g_ref[...], wog_ref[...])
    hb = _rms(x, lnf_ref[...]).astype(BF16)
    d_ff = wgate_ref.shape[1]
    ffn = None
    for j in range(d_ff // FF_TILE):
        cs = slice(j * FF_TILE, (j + 1) * FF_TILE)
        gate = _dot(hb, wgate_ref[:, cs])
        up = _dot(hb, wup_ref[:, cs])
        act = (gate * _sigmoid(gate)) * up
        down = _dot(act.astype(BF16), wdown_ref[cs, :])
        ffn = down if ffn is None else ffn + down
    x = x + ffn
    hb = _rms(x, lnp_ref[...]).astype(BF16)
    x = x + _sigmoid(_dot(hb, wpg_ref[...])) * _dot(p_ref[...].astype(BF16), wpp_ref[...])
    y_ref[...] = _rms(x, fng_ref[...])


def _tail(x2, r2, g2, p2, wor, wog, lnf, wgate, wup, wdown, lnp, wpg, wpp, fng, tm):
    n, d = x2.shape
    tok = lambda w: pl.BlockSpec((tm, w), lambda i: (i, 0))
    consts = (wor, wog, lnf, wgate, wup, wdown, lnp, wpg, wpp, fng)
    return pl.pallas_call(
        _tail_kernel,
        grid=(n // tm,),
        in_specs=[tok(d), tok(r2.shape[1]), tok(g2.shape[1]), tok(p2.shape[1])]
                 + [_resident(a.shape) for a in consts],
        out_specs=tok(d),
        out_shape=jax.ShapeDtypeStruct((n, d), F32),
        compiler_params=pltpu.CompilerParams(dimension_semantics=("arbitrary",),
                                             vmem_limit_bytes=VMEM_LIMIT_BYTES),
        name="tail",
    )(x2, r2, g2, p2, *consts)


def _pad_rows(a, rows):
    return jnp.concatenate([a, jnp.zeros((rows - a.shape[0],) + a.shape[1:], a.dtype)], axis=0)


def _pad_lanes(a, lanes, offset=0):
    out = jnp.zeros(a.shape[:-1] + (lanes,), a.dtype)
    return lax.dynamic_update_slice_in_dim(out, a, offset, axis=a.ndim - 1)


def _layer(x, p, shift0, wkv0, conv0, gdn0, wts, final_g):
    (ln_mix_g, w_in, mu_shift, w0, w_lora_up, a0, a_lora_up, g_lora_up, k_k, k_a, r_k, ln_x_w, ln_x_b,
     conv_w, a_log, dt_bias, gdn_norm_g, w_out, ln_ffn_g, w_gate, w_up, w_down, ln_ple_g,
     w_ple_gate, w_ple_proj) = wts
    b, l, d = x.shape
    n = b * l
    tm = min(256, n)
    row = lambda a: a.reshape(1, -1).astype(F32)

    off_z = R_PROJ + G_QKV
    off_b = off_z + D_G
    w_in_b = w_in.astype(BF16)
    wr, wq, wz = w_in_b[:, :R_PROJ], w_in_b[:, R_PROJ:off_z], w_in_b[:, off_z:off_b]
    wb = _pad_lanes(w_in_b[:, off_b:], LANES)
    x2 = x.reshape(n, d)
    f2, qkv2, z2, ba2 = _inproj(x2, row(ln_mix_g), wr, wq, wz, wb, tm)

    zeros_lora = jnp.zeros((LORA_W, D_R), BF16)
    wlo = jnp.concatenate([w_lora_up.astype(BF16), zeros_lora], axis=0)
    alo = jnp.concatenate([zeros_lora, a_lora_up.astype(BF16)], axis=0)
    head_id = jnp.arange(D_R) // R_HEAD_DIM
    seg = (head_id[:, None] == head_id[None, :]).astype(BF16)
    eye = jnp.eye(D_R, dtype=BF16)
    f3 = f2.reshape(b, l, R_PROJ)
    r_out, wkv1_kv = _rwkv(f3, shift0.reshape(b, 1, R_PROJ), jnp.swapaxes(wkv0, -1, -2),
                           row(mu_shift), row(w0), wlo, alo, row(a0), g_lora_up.astype(BF16),
                           row(k_k), row(k_a), row(r_k), row(ln_x_w), row(ln_x_b), seg, eye)
    wkv1 = jnp.swapaxes(wkv1_kv, -1, -2)
    shift1 = f3[:, -1]

    qkv3 = qkv2.reshape(b, l, G_QKV)
    g_out, gdn1 = _gdn(qkv3, z2.reshape(b, l, D_G), ba2.reshape(b, l, LANES), conv0, gdn0,
                       conv_w, _pad_lanes(row(a_log), LANES, G_HEADS),
                       _pad_lanes(row(dt_bias), LANES, G_HEADS), row(gdn_norm_g), eye)
    conv1 = jnp.concatenate([conv0, qkv3], axis=1)[:, l:]

    w_out_b = w_out.astype(BF16)
    y2 = _tail(x2, r_out.reshape(n, D_R), g_out.reshape(n, D_G), p.reshape(n, -1),
               w_out_b[:D_R], w_out_b[D_R:], row(ln_ffn_g), w_gate.astype(BF16), w_up.astype(BF16),
               w_down.astype(BF16), row(ln_ple_g), w_ple_gate.astype(BF16), w_ple_proj.astype(BF16),
               row(final_g), tm)
    return y2.reshape(b, l, d), shift1, wkv1, conv1, gdn1


def kernel(x_prompt, x_sample, state_shift, state_wkv, state_conv, state_gdn, p_prompt, p_sample,
           ln_mix_g, w_in, mu_shift, w0, w_lora_up, a0, a_lora_up, g_lora_up, k_k, k_a, r_k,
           ln_x_w, ln_x_b, conv_w, a_log, dt_bias, gdn_norm_g, w_out, ln_ffn_g, w_gate, w_up,
           w_down, ln_ple_g, w_ple_gate, w_ple_proj, final_norm_g):
    depth = w_in.shape[0]
    assert depth == 1, "the final norm is fused into the single layer's tail kernel"
    bp = x_prompt.shape[0]
    dt = x_prompt.dtype
    wts = tuple(a[0] for a in (ln_mix_g, w_in, mu_shift, w0, w_lora_up, a0, a_lora_up, g_lora_up, k_k,
                               k_a, r_k, ln_x_w, ln_x_b, conv_w, a_log, dt_bias, gdn_norm_g, w_out,
                               ln_ffn_g, w_gate, w_up, w_down, ln_ple_g, w_ple_gate, w_ple_proj))
    z_shift = jnp.zeros((bp, R_PROJ), dt)
    z_wkv = jnp.zeros((bp, R_HEADS, R_HEAD_DIM, R_HEAD_DIM), dt)
    z_conv = jnp.zeros((bp, CONV_W - 1, G_QKV), dt)
    z_gdn = jnp.zeros((bp, G_HEADS, G_HEAD_DIM, G_HEAD_DIM), dt)
    yp, a1, a2, a3, a4 = _layer(x_prompt, p_prompt[0], z_shift, z_wkv, z_conv, z_gdn, wts, final_norm_g)
    ys, b1, b2, b3, b4 = _layer(x_sample, p_sample[0], state_shift[0], state_wkv[0], state_conv[0],
                                state_gdn[0], wts, final_norm_g)
    stack = lambda a: a[None]
    return (yp, ys, stack(a1), stack(a2), stack(a3), stack(a4),
            stack(b1), stack(b2), stack(b3), stack(b4))
```

```python
import functools

import jax
import jax.numpy as jnp
from jax import lax
from jax.experimental import pallas as pl
from jax.experimental.pallas import tpu as pltpu

F32 = jnp.float32
BF16 = jnp.bfloat16

R_HEADS = 8
R_HEAD_DIM = 64
D_R = R_HEADS * R_HEAD_DIM
LORA_W = 64
LORA_A = 64
LORA_G = 128
R_PROJ = 3 * D_R + LORA_W + LORA_A + LORA_G
G_HEADS = 4
G_HEAD_DIM = 128
D_G = G_HEADS * G_HEAD_DIM
CONV_W = 4
G_QKV = 3 * D_G
CHUNK = 64
RMS_EPS = 1e-6
GN_EPS = 64e-5

LANES = 128
SUBLANES = 8
VMEM_LIMIT_BYTES = 56 * 1024 * 1024
INV_BLOCK = 16
FF_TILE = 256
GDN_BATCH_BLOCK = 4
RWKV_BATCH_BLOCK = 4
R_PAIR = LANES // R_HEAD_DIM
R_PAIRS = R_HEADS // R_PAIR


def _dot(a, b, nt=False):
    dn = (((1,), (1,)), ((), ())) if nt else (((1,), (0,)), ((), ()))
    return lax.dot_general(a, b, dn, preferred_element_type=F32)


def _split(a):
    hi = a.astype(BF16)
    lo = (a - hi.astype(F32)).astype(BF16)
    return hi, lo


def _mm1(a, b, nt=False):
    return _dot(a.astype(BF16), b.astype(BF16), nt)


def _mm_exact_lhs(a_bf16, b):
    b1 = b.astype(BF16)
    r1 = b - b1.astype(F32)
    b2 = r1.astype(BF16)
    b3 = (r1 - b2.astype(F32)).astype(BF16)
    return _dot(a_bf16, b1) + (_dot(a_bf16, b2) + _dot(a_bf16, b3))


def _sigmoid(x):
    return 1.0 / (1.0 + jnp.exp(-x))


def _softplus(x):
    return jnp.maximum(x, 0.0) + jnp.log1p(jnp.exp(-jnp.abs(x)))


def _iota2(shape):
    return (lax.broadcasted_iota(jnp.int32, shape, 0), lax.broadcasted_iota(jnp.int32, shape, 1))


def _tri_inv(a_list, mm):
    t, width = a_list[0].shape
    assert t <= 4 * INV_BLOCK and width % t == 0
    row, col = _iota2((t, width))
    col = col % t
    eye = (row == col).astype(F32)
    if t > INV_BLOCK:
        blk = (row // INV_BLOCK) == (col // INV_BLOCK)
        d = [jnp.where(blk, a, 0.0) for a in a_list]
    else:
        d = a_list
    d2 = [mm(di, di) for di in d]
    x = [eye - di for di in d]
    d4 = [mm(p, p) for p in d2]
    x = [xi + mm(xi, p) for xi, p in zip(x, d2)]
    d8 = [mm(p, p) for p in d4]
    x = [xi + mm(xi, p) for xi, p in zip(x, d4)]
    x = [xi + mm(xi, p) for xi, p in zip(x, d8)]
    if t <= INV_BLOCK:
        return x
    m = [mm(xi, a - di) for xi, a, di in zip(x, a_list, d)]
    m2 = [mm(mi, mi) for mi in m]
    n = [eye - mi for mi in m]
    n = [ni + mm(ni, p) for ni, p in zip(n, m2)]
    return [mm(ni, xi) for ni, xi in zip(n, x)]


def _resident(shape):
    nd = len(shape)
    return pl.BlockSpec(shape, lambda *_: (0,) * nd)


def _inproj_kernel(x_ref, g_ref, wr_ref, wq_ref, wz_ref, wb_ref, f_ref, q_ref, z_ref, b_ref):
    x = x_ref[...]
    h = x * lax.rsqrt(jnp.mean(x * x, axis=-1, keepdims=True) + RMS_EPS) * g_ref[...]
    hb = h.astype(BF16)
    f_ref[...] = _dot(hb, wr_ref[...])
    q_ref[...] = _dot(hb, wq_ref[...])
    z_ref[...] = _dot(hb, wz_ref[...])
    b_ref[...] = _dot(hb, wb_ref[...])


def _inproj(x2, g, wr, wq, wz, wb, tm):
    n, d = x2.shape
    outs = (R_PROJ, G_QKV, D_G, LANES)
    return pl.pallas_call(
        _inproj_kernel,
        grid=(n // tm,),
        in_specs=[pl.BlockSpec((tm, d), lambda i: (i, 0)), _resident(g.shape),
                  _resident(wr.shape), _resident(wq.shape), _resident(wz.shape), _resident(wb.shape)],
        out_specs=[pl.BlockSpec((tm, w), lambda i: (i, 0)) for w in outs],
        out_shape=[jax.ShapeDtypeStruct((n, w), F32) for w in outs],
        compiler_params=pltpu.CompilerParams(dimension_semantics=("arbitrary",),
                                             vmem_limit_bytes=VMEM_LIMIT_BYTES),
        name="inproj",
    )(x2, g, wr, wq, wz, wb)


def _rwkv_kernel(t, f_ref, shift_ref, s0_ref, mu_ref, w0_ref, wlo_ref, alo_ref, a0_ref, gup_ref,
                 kk_ref, ka_ref, rk_ref, lnw_ref, lnb_ref, seg_ref,
                 out_ref, sout_ref, fbuf, state, ybuf):
    c_idx = pl.program_id(1)
    base = SUBLANES
    hd = R_HEAD_DIM
    pw = R_PAIR * hd

    bb = f_ref.shape[0]

    @pl.when(c_idx == 0)
    def _():
        fbuf[:, base - 1:base, :] = shift_ref[...]
        state[...] = jnp.zeros(state.shape, F32)
        for bi in range(bb):
            for h in range(R_HEADS):
                o = (h % R_PAIR) * hd
                state[bi, h // R_PAIR, o:o + hd, o:o + hd] = s0_ref[bi, h]

    seg = seg_ref[...]

    def segsum(*xs):
        parts = [piece for x in xs for piece in _split(x)]
        o = _dot(jnp.concatenate(parts, axis=0), seg)
        return [o[2 * i * t:(2 * i + 1) * t] + o[(2 * i + 1) * t:(2 * i + 2) * t] for i in range(len(xs))]

    row_t, col_t = _iota2((t, t))
    tri = (row_t >= col_t).astype(BF16)

    def prep(bi):
        f = f_ref[bi]
        fbuf[bi, base:base + t, :] = f
        fp = fbuf[bi, base - 1:base - 1 + t, :]
        fbuf[bi, base - 1:base, :] = f[t - 1:t, :]
        fm = f + (fp - f) * mu_ref[...]

        r = fm[:, 0:D_R]
        k = fm[:, D_R:2 * D_R]
        v = fm[:, 2 * D_R:3 * D_R]
        wa = fm[:, 3 * D_R:3 * D_R + LORA_W + LORA_A]
        gl = fm[:, 3 * D_R + LORA_W + LORA_A:]

        w_log = -_softplus(-(w0_ref[...] + _mm1(jnp.tanh(wa), wlo_ref[...]))) - 0.5
        ld = -jnp.exp(w_log)
        a = _sigmoid(a0_ref[...] + _mm1(wa, alo_ref[...]))
        g = _mm1(_sigmoid(gl), gup_ref[...])
        kk = k * kk_ref[...]
        keff = k * (1.0 + (a - 1.0) * ka_ref[...])
        kk_ss, bonus_dot = segsum(kk * kk, r * keff * rk_ref[...])
        kk = kk * lax.rsqrt(kk_ss + 1e-12)
        kka = kk * a

        cum = _mm_exact_lhs(tri, ld)
        cl = cum[t - 1:t, :]
        e_mc = jnp.exp(-cum)
        e_lc = jnp.exp(cl - cum)
        stack_t = jnp.concatenate([keff * e_lc, kka * e_lc, jnp.broadcast_to(cl, (SUBLANES, D_R))],
                                  axis=0).T
        return dict(
            rt=(r * jnp.exp(cum)).astype(BF16), kpt=(kk * jnp.exp(cum - ld)).astype(BF16),
            bt=(kka * e_mc).astype(BF16), ktl=(keff * e_mc).astype(BF16), vb=v.astype(BF16),
            kb_t=stack_t[:, :2 * t].astype(BF16),
            pt_col=jnp.exp(stack_t[:, 2 * t:2 * t + 1]),
            gate=g, bonus=bonus_dot * v)

    pre = [prep(bi) for bi in range(bb)]

    row_c, col_c = _iota2((R_PAIR * t, pw))
    m_ch = (row_c // t) == (col_c // hd)
    row_a, col_a = _iota2((R_PAIR * t, R_PAIR * t))
    m_a = (row_a // t) == (col_a // t)
    row_s, col_s = _iota2((pw, pw))
    m_s = (row_s // hd) == (col_s // hd)
    row_g, col_g = _iota2((t, 2 * R_PAIR * t))
    col_g = col_g % t
    strict_g = row_g > col_g
    incl_g = row_g >= col_g
    zero_b = jnp.zeros((), BF16)

    def bd(x, mask):
        return jnp.where(mask, jnp.concatenate([x] * R_PAIR, axis=0), zero_b)

    def mm_pair(x, y):
        return _dot(x.astype(BF16), bd(y.astype(BF16), m_a))

    units = [(bi, p) for bi in range(bb) for p in range(R_PAIRS)]
    n_u = range(len(units))
    pair = lambda name, u: pre[u[0]][name][:, u[1] * pw:(u[1] + 1) * pw]
    v_bd = [bd(pair("vb", u), m_ch) for u in units]
    lhs = [jnp.concatenate([pair("kpt", u), pair("rt", u)], axis=0) for u in units]
    rhs = [jnp.concatenate([bd(pair("bt", u), m_ch), bd(pair("ktl", u), m_ch)], axis=0)
           for u in units]
    gm = [_dot(lhs[i], rhs[i], nt=True) for i in n_u]
    top = [jnp.where(strict_g, gm[i][:t], 0.0) for i in n_u]
    bot = [jnp.where(incl_g, gm[i][t:], 0.0).astype(BF16) for i in n_u]
    av = [_dot(top[i][:, R_PAIR * t:].astype(BF16), v_bd[i]) for i in n_u]
    tinv = _tri_inv([top[i][:, :R_PAIR * t] for i in n_u], mm_pair)
    wu = [_dot(tinv[i].astype(BF16),
               jnp.concatenate([bd(pair("kpt", u), m_ch), bd(av[i].astype(BF16), m_ch)], axis=1))
          for i, u in enumerate(units)]
    s = [state[u[0], u[1]] for u in units]
    ws = [_dot(jnp.concatenate([wu[i][:, :pw].astype(BF16), lhs[i][t:]], axis=0), s[i].astype(BF16))
          for i in n_u]
    zb = [(ws[i][:t] + wu[i][:, pw:]).astype(BF16) for i in n_u]
    yz = [_dot(bot[i], jnp.concatenate([bd(-zb[i], m_ch), v_bd[i]], axis=0)) for i in n_u]
    upd = [_dot(pre[u[0]]["kb_t"][u[1] * pw:(u[1] + 1) * pw],
                jnp.concatenate([pair("vb", u), -zb[i]], axis=0)) for i, u in enumerate(units)]
    for i, (bi, p) in enumerate(units):
        state[bi, p] = s[i] * pre[bi]["pt_col"][p * pw:(p + 1) * pw] + jnp.where(m_s, upd[i], 0.0)
        ybuf[bi, :, p * pw:(p + 1) * pw] = ws[i][t:] + yz[i]

    inv_n = 1.0 / hd
    for bi in range(bb):
        y = ybuf[bi]
        dlt = y - segsum(y)[0] * inv_n
        var = segsum(dlt * dlt)[0] * inv_n
        yn = dlt * lax.rsqrt(var + GN_EPS) * lnw_ref[...] + lnb_ref[...]
        out_ref[bi] = ((yn + pre[bi]["bonus"]) * pre[bi]["gate"]).astype(out_ref.dtype)

    @pl.when(c_idx == pl.num_programs(1) - 1)
    def _():
        for bi in range(bb):
            for h in range(R_HEADS):
                o = (h % R_PAIR) * hd
                sout_ref[bi, h] = state[bi, h // R_PAIR, o:o + hd, o:o + hd]


def _rwkv(f3, shift0, s0_kv, mu, w0, wlo, alo, a0, gup, k_k, k_a, r_k, lnw, lnb, seg):
    b, l, _ = f3.shape
    t = min(CHUNK, l)
    nc = l // t
    bb = RWKV_BATCH_BLOCK if b % RWKV_BATCH_BLOCK == 0 else 1
    row = lambda a: _resident(a.shape)
    return pl.pallas_call(
        functools.partial(_rwkv_kernel, t),
        grid=(b // bb, nc),
        in_specs=[pl.BlockSpec((bb, t, R_PROJ), lambda i, c: (i, c, 0)),
                  pl.BlockSpec((bb, 1, R_PROJ), lambda i, c: (i, 0, 0)),
                  pl.BlockSpec((bb, R_HEADS, R_HEAD_DIM, R_HEAD_DIM), lambda i, c: (i, 0, 0, 0)),
                  row(mu), row(w0), row(wlo), row(alo), row(a0), row(gup),
                  row(k_k), row(k_a), row(r_k), row(lnw), row(lnb), row(seg)],
        out_specs=[pl.BlockSpec((bb, t, D_R), lambda i, c: (i, c, 0)),
                   pl.BlockSpec((bb, R_HEADS, R_HEAD_DIM, R_HEAD_DIM), lambda i, c: (i, 0, 0, 0))],
        out_shape=[jax.ShapeDtypeStruct((b, l, D_R), BF16),
                   jax.ShapeDtypeStruct((b, R_HEADS, R_HEAD_DIM, R_HEAD_DIM), F32)],
        scratch_shapes=[pltpu.VMEM((bb, SUBLANES + t, R_PROJ), F32),
                        pltpu.VMEM((bb, R_PAIRS, R_PAIR * R_HEAD_DIM, R_PAIR * R_HEAD_DIM), F32),
                        pltpu.VMEM((bb, t, D_R), F32)],
        compiler_params=pltpu.CompilerParams(dimension_semantics=("arbitrary", "arbitrary"),
                                             vmem_limit_bytes=VMEM_LIMIT_BYTES),
        name="rwkv7",
    )(f3, shift0, s0_kv, mu, w0, wlo, alo, a0, gup, k_k, k_a, r_k, lnw, lnb, seg)


def _gdn_kernel(t, qkv_ref, z_ref, ba_ref, conv0_ref, s0_ref, cw_ref, alog_ref, dtb_ref, ng_ref,
                out_ref, sout_ref, cbuf, state):
    c_idx = pl.program_id(1)
    base = SUBLANES
    hist = CONV_W - 1
    bb = qkv_ref.shape[0]

    @pl.when(c_idx == 0)
    def _():
        cbuf[:, base - hist:base, :] = conv0_ref[...]
        state[...] = s0_ref[...]

    row, col = _iota2((t, t))
    incl = row >= col
    strict = row > col
    tri = incl.astype(BF16)
    mm = _mm1

    units = [(bi, h) for bi in range(bb) for h in range(G_HEADS)]
    qh, kh, vh, bcol, gcol, grow, glast, kt = {}, {}, {}, {}, {}, {}, {}, {}
    for bi in range(bb):
        qkv = qkv_ref[bi]
        cbuf[bi, base:base + t, :] = qkv
        conv = qkv * cw_ref[hist:hist + 1, :]
        for j in range(hist):
            conv = conv + cbuf[bi, base - hist + j:base - hist + j + t, :] * cw_ref[j:j + 1, :]
        cbuf[bi, base - hist:base, :] = qkv[t - hist:t, :]
        conv = conv * _sigmoid(conv)

        ba = ba_ref[bi]
        beta_t = _sigmoid(ba)
        glog_t = -jnp.exp(alog_ref[...]) * _softplus(ba + dtb_ref[...])
        g_t = _mm_exact_lhs(tri, glog_t)
        g_tt = g_t.T
        k_norm = []
        for h in range(G_HEADS):
            sl = slice(h * G_HEAD_DIM, (h + 1) * G_HEAD_DIM)
            q = conv[:, sl]
            qh[bi, h] = q * lax.rsqrt(jnp.sum(q * q, axis=-1, keepdims=True) + 1e-6) * (G_HEAD_DIM ** -0.5)
            k = conv[:, D_G + h * G_HEAD_DIM:D_G + (h + 1) * G_HEAD_DIM]
            kh[bi, h] = k * lax.rsqrt(jnp.sum(k * k, axis=-1, keepdims=True) + 1e-6)
            k_norm.append(kh[bi, h])
            vh[bi, h] = conv[:, 2 * D_G + h * G_HEAD_DIM:2 * D_G + (h + 1) * G_HEAD_DIM]
            bcol[bi, h] = beta_t[:, h:h + 1]
            gcol[bi, h] = g_t[:, G_HEADS + h:G_HEADS + h + 1]
            grow[bi, h] = g_tt[G_HEADS + h:G_HEADS + h + 1, :]
            glast[bi, h] = gcol[bi, h][t - 1:t, :]
        kt_all = jnp.concatenate(k_norm, axis=1).T
        for h in range(G_HEADS):
            kt[bi, h] = kt_all[h * G_HEAD_DIM:(h + 1) * G_HEAD_DIM]

    dec_incl, kb, eg = {}, {}, {}
    for u_ in units:
        dg = gcol[u_] - grow[u_]
        dec_incl[u_] = jnp.where(incl, jnp.exp(jnp.where(incl, dg, 0.0)), 0.0)
        kb[u_] = kh[u_] * bcol[u_]
        eg[u_] = jnp.exp(gcol[u_])
    a_mat = [mm(kb[u_], kh[u_], nt=True) * jnp.where(strict, dec_incl[u_], 0.0) for u_ in units]
    qk = [mm(qh[u_], kh[u_], nt=True) * dec_incl[u_] for u_ in units]
    tinv = _tri_inv(a_mat, mm)
    uw = [mm(tinv[i], jnp.concatenate([vh[u_] * bcol[u_], kb[u_] * eg[u_]], axis=1))
          for i, u_ in enumerate(units)]
    s = [state[u_[0], u_[1]] for u_ in units]
    wqs = [mm(jnp.concatenate([uw[i][:, G_HEAD_DIM:], qh[u_] * eg[u_]], axis=0), s[i])
           for i, u_ in enumerate(units)]
    v_new = [uw[i][:, :G_HEAD_DIM] - wqs[i][:t] for i in range(len(units))]
    qkv_new = [mm(qk[i], v_new[i]) for i in range(len(units))]
    kdv = [mm(kt[u_] * jnp.exp(glast[u_] - grow[u_]), v_new[i]) for i, u_ in enumerate(units)]
    for i, (bi, h) in enumerate(units):
        sl = slice(h * G_HEAD_DIM, (h + 1) * G_HEAD_DIM)
        state[bi, h] = s[i] * jnp.exp(glast[bi, h]) + kdv[i]
        o = wqs[i][t:] + qkv_new[i]
        zh = z_ref[bi, :, sl]
        o = (o * lax.rsqrt(jnp.mean(o * o, axis=-1, keepdims=True) + RMS_EPS) * ng_ref[...]
             * (zh * _sigmoid(zh)))
        out_ref[bi, :, sl] = o.astype(out_ref.dtype)

    @pl.when(c_idx == pl.num_programs(1) - 1)
    def _():
        sout_ref[...] = state[...]


def _gdn(qkv3, z3, ba3, conv0, s0, cw, alog, dtb, ng):
    b, l, _ = qkv3.shape
    t = min(CHUNK, l)
    nc = l // t
    bb = GDN_BATCH_BLOCK if b % GDN_BATCH_BLOCK == 0 else 1
    row = lambda a: _resident(a.shape)
    return pl.pallas_call(
        functools.partial(_gdn_kernel, t),
        grid=(b // bb, nc),
        in_specs=[pl.BlockSpec((bb, t, G_QKV), lambda i, c: (i, c, 0)),
                  pl.BlockSpec((bb, t, D_G), lambda i, c: (i, c, 0)),
                  pl.BlockSpec((bb, t, LANES), lambda i, c: (i, c, 0)),
                  pl.BlockSpec((bb, CONV_W - 1, G_QKV), lambda i, c: (i, 0, 0)),
                  pl.BlockSpec((bb, G_HEADS, G_HEAD_DIM, G_HEAD_DIM), lambda i, c: (i, 0, 0, 0)),
                  row(cw), row(alog), row(dtb), row(ng)],
        out_specs=[pl.BlockSpec((bb, t, D_G), lambda i, c: (i, c, 0)),
                   pl.BlockSpec((bb, G_HEADS, G_HEAD_DIM, G_HEAD_DIM), lambda i, c: (i, 0, 0, 0))],
        out_shape=[jax.ShapeDtypeStruct((b, l, D_G), BF16),
                   jax.ShapeDtypeStruct((b, G_HEADS, G_HEAD_DIM, G_HEAD_DIM), F32)],
        scratch_shapes=[pltpu.VMEM((bb, SUBLANES + t, G_QKV), F32),
                        pltpu.VMEM((bb, G_HEADS, G_HEAD_DIM, G_HEAD_DIM), F32)],
        compiler_params=pltpu.CompilerParams(dimension_semantics=("arbitrary", "arbitrary"),
                                             vmem_limit_bytes=VMEM_LIMIT_BYTES),
        name="gdn",
    )(qkv3, z3, ba3, conv0, s0, cw, alog, dtb, ng)


def _rms(x, g):
    return x * lax.rsqrt(jnp.mean(x * x, axis=-1, keepdims=True) + RMS_EPS) * g


def _tail_kernel(x_ref, r_ref, g_ref, p_ref, wor_ref, wog_ref, lnf_ref, wgate_ref, wup_ref, wdown_ref,
                 lnp_ref, wpg_ref, wpp_ref, fng_ref, y_ref):
    x = x_ref[...] + _dot(r_ref[...], wor_ref[...]) + _dot(g_ref[...], wog_ref[...])
    hb = _rms(x, lnf_ref[...]).astype(BF16)
    d_ff = wgate_ref.shape[1]
    ffn = None
    for j in range(d_ff // FF_TILE):
        cs = slice(j * FF_TILE, (j + 1) * FF_TILE)
        gate = _dot(hb, wgate_ref[:, cs])
        up = _dot(hb, wup_ref[:, cs])
        act = (gate * _sigmoid(gate)) * up
        down = _dot(act.astype(BF16), wdown_ref[cs, :])
        ffn = down if ffn is None else ffn + down
    x = x + ffn
    hb = _rms(x, lnp_ref[...]).astype(BF16)
    x = x + _sigmoid(_dot(hb, wpg_ref[...])) * _dot(p_ref[...].astype(BF16), wpp_ref[...])
    y_ref[...] = _rms(x, fng_ref[...])


def _tail(x2, r2, g2, p2, wor, wog, lnf, wgate, wup, wdown, lnp, wpg, wpp, fng, tm):
    n, d = x2.shape
    tok = lambda w: pl.BlockSpec((tm, w), lambda i: (i, 0))
    consts = (wor, wog, lnf, wgate, wup, wdown, lnp, wpg, wpp, fng)
    return pl.pallas_call(
        _tail_kernel,
        grid=(n // tm,),
        in_specs=[tok(d), tok(r2.shape[1]), tok(g2.shape[1]), tok(p2.shape[1])]
                 + [_resident(a.shape) for a in consts],
        out_specs=tok(d),
        out_shape=jax.ShapeDtypeStruct((n, d), F32),
        compiler_params=pltpu.CompilerParams(dimension_semantics=("arbitrary",),
                                             vmem_limit_bytes=VMEM_LIMIT_BYTES),
        name="tail",
    )(x2, r2, g2, p2, *consts)


def _pad_lanes(a, lanes, offset=0):
    out = jnp.zeros(a.shape[:-1] + (lanes,), a.dtype)
    return lax.dynamic_update_slice_in_dim(out, a, offset, axis=a.ndim - 1)


def _layer(x, p, shift0, wkv0, conv0, gdn0, wts, final_g):
    (ln_mix_g, w_in, mu_shift, w0, w_lora_up, a0, a_lora_up, g_lora_up, k_k, k_a, r_k, ln_x_w, ln_x_b,
     conv_w, a_log, dt_bias, gdn_norm_g, w_out, ln_ffn_g, w_gate, w_up, w_down, ln_ple_g,
     w_ple_gate, w_ple_proj) = wts
    b, l, d = x.shape
    n = b * l
    tm = min(256, n)
    row = lambda a: a.reshape(1, -1).astype(F32)

    off_z = R_PROJ + G_QKV
    off_b = off_z + D_G
    w_in_b = w_in.astype(BF16)
    wr, wq, wz = w_in_b[:, :R_PROJ], w_in_b[:, R_PROJ:off_z], w_in_b[:, off_z:off_b]
    wb = _pad_lanes(w_in_b[:, off_b:], LANES)
    x2 = x.reshape(n, d)
    f2, qkv2, z2, ba2 = _inproj(x2, row(ln_mix_g), wr, wq, wz, wb, tm)

    zeros_lora = jnp.zeros((LORA_W, D_R), BF16)
    wlo = jnp.concatenate([w_lora_up.astype(BF16), zeros_lora], axis=0)
    alo = jnp.concatenate([zeros_lora, a_lora_up.astype(BF16)], axis=0)
    head_id = jnp.arange(D_R) // R_HEAD_DIM
    seg = (head_id[:, None] == head_id[None, :]).astype(BF16)
    f3 = f2.reshape(b, l, R_PROJ)
    r_out, wkv1_kv = _rwkv(f3, shift0.reshape(b, 1, R_PROJ), jnp.swapaxes(wkv0, -1, -2),
                           row(mu_shift), row(w0), wlo, alo, row(a0), g_lora_up.astype(BF16),
                           row(k_k), row(k_a), row(r_k), row(ln_x_w), row(ln_x_b), seg)
    wkv1 = jnp.swapaxes(wkv1_kv, -1, -2)
    shift1 = f3[:, -1]

    qkv3 = qkv2.reshape(b, l, G_QKV)
    g_out, gdn1 = _gdn(qkv3, z2.reshape(b, l, D_G), ba2.reshape(b, l, LANES), conv0, gdn0,
                       conv_w, _pad_lanes(row(a_log), LANES, G_HEADS),
                       _pad_lanes(row(dt_bias), LANES, G_HEADS), row(gdn_norm_g))
    conv1 = jnp.concatenate([conv0, qkv3], axis=1)[:, l:]

    w_out_b = w_out.astype(BF16)
    y2 = _tail(x2, r_out.reshape(n, D_R), g_out.reshape(n, D_G), p.reshape(n, -1),
               w_out_b[:D_R], w_out_b[D_R:], row(ln_ffn_g), w_gate.astype(BF16), w_up.astype(BF16),
               w_down.astype(BF16), row(ln_ple_g), w_ple_gate.astype(BF16), w_ple_proj.astype(BF16),
               row(final_g), tm)
    return y2.reshape(b, l, d), shift1, wkv1, conv1, gdn1


def kernel(x_prompt, x_sample, state_shift, state_wkv, state_conv, state_gdn, p_prompt, p_sample,
           ln_mix_g, w_in, mu_shift, w0, w_lora_up, a0, a_lora_up, g_lora_up, k_k, k_a, r_k,
           ln_x_w, ln_x_b, conv_w, a_log, dt_bias, gdn_norm_g, w_out, ln_ffn_g, w_gate, w_up,
           w_down, ln_ple_g, w_ple_gate, w_ple_proj, final_norm_g):
    depth = w_in.shape[0]
    assert depth == 1, "the final norm is fused into the single layer's tail kernel"
    bp = x_prompt.shape[0]
    dt = x_prompt.dtype
    wts = tuple(a[0] for a in (ln_mix_g, w_in, mu_shift, w0, w_lora_up, a0, a_lora_up, g_lora_up, k_k,
                               k_a, r_k, ln_x_w, ln_x_b, conv_w, a_log, dt_bias, gdn_norm_g, w_out,
                               ln_ffn_g, w_gate, w_up, w_down, ln_ple_g, w_ple_gate, w_ple_proj))
    z_shift = jnp.zeros((bp, R_PROJ), dt)
    z_wkv = jnp.zeros((bp, R_HEADS, R_HEAD_DIM, R_HEAD_DIM), dt)
    z_conv = jnp.zeros((bp, CONV_W - 1, G_QKV), dt)
    z_gdn = jnp.zeros((bp, G_HEADS, G_HEAD_DIM, G_HEAD_DIM), dt)
    yp, a1, a2, a3, a4 = _layer(x_prompt, p_prompt[0], z_shift, z_wkv, z_conv, z_gdn, wts, final_norm_g)
    ys, b1, b2, b3, b4 = _layer(x_sample, p_sample[0], state_shift[0], state_wkv[0], state_conv[0],
                                state_gdn[0], wts, final_norm_g)
    stack = lambda a: a[None]
    return (yp, ys, stack(a1), stack(a2), stack(a3), stack(a4),
            stack(b1), stack(b2), stack(b3), stack(b4))
```

```python
import functools
import math

import jax
import jax.numpy as jnp
from jax import lax
from jax.experimental import pallas as pl
from jax.experimental.pallas import tpu as pltpu

F32 = jnp.float32
BF16 = jnp.bfloat16

R_HEADS = 8
R_HEAD_DIM = 64
D_R = R_HEADS * R_HEAD_DIM
LORA_W = 64
LORA_A = 64
LORA_G = 128
R_PROJ = 3 * D_R + LORA_W + LORA_A + LORA_G
G_HEADS = 4
G_HEAD_DIM = 128
D_G = G_HEADS * G_HEAD_DIM
CONV_W = 4
G_QKV = 3 * D_G
CHUNK = 64
RMS_EPS = 1e-6
GN_EPS = 64e-5
DECAY_SCALE = math.exp(-0.5)

LANES = 128
SUBLANES = 8
VMEM_LIMIT_BYTES = 56 * 1024 * 1024
INV_BLOCK = 16
FF_TILE = 256
TOKEN_TILE = 512
GDN_BATCH_BLOCK = 4
RWKV_BATCH_BLOCK = 4
R_PAIR = LANES // R_HEAD_DIM
R_PAIRS = R_HEADS // R_PAIR


def _dot(a, b, nt=False):
    dn = (((1,), (1,)), ((), ())) if nt else (((1,), (0,)), ((), ()))
    return lax.dot_general(a, b, dn, preferred_element_type=F32)


def _split(a):
    hi = a.astype(BF16)
    lo = (a - hi.astype(F32)).astype(BF16)
    return hi, lo


def _mm1(a, b, nt=False):
    return _dot(a.astype(BF16), b.astype(BF16), nt)


def _mm_exact_lhs(a_bf16, b):
    b1 = b.astype(BF16)
    r1 = b - b1.astype(F32)
    b2 = r1.astype(BF16)
    b3 = (r1 - b2.astype(F32)).astype(BF16)
    return _dot(a_bf16, b1) + (_dot(a_bf16, b2) + _dot(a_bf16, b3))


def _sigmoid(x):
    return 0.5 * jnp.tanh(0.5 * x) + 0.5


def _softplus(x):
    return jnp.maximum(x, 0.0) + jnp.log1p(jnp.exp(-jnp.abs(x)))


def _iota2(shape):
    return (lax.broadcasted_iota(jnp.int32, shape, 0), lax.broadcasted_iota(jnp.int32, shape, 1))


def _tri_inv(a_list, mm):
    t, width = a_list[0].shape
    assert t <= 4 * INV_BLOCK and width % t == 0
    row, col = _iota2((t, width))
    col = col % t
    eye = (row == col).astype(F32)
    if t > INV_BLOCK:
        blk = (row // INV_BLOCK) == (col // INV_BLOCK)
        d = [jnp.where(blk, a, 0.0) for a in a_list]
    else:
        d = a_list
    d2 = [mm(di, di) for di in d]
    x = [eye - di for di in d]
    d4 = [mm(p, p) for p in d2]
    x = [xi + mm(xi, p) for xi, p in zip(x, d2)]
    d8 = [mm(p, p) for p in d4]
    x = [xi + mm(xi, p) for xi, p in zip(x, d4)]
    x = [xi + mm(xi, p) for xi, p in zip(x, d8)]
    if t <= INV_BLOCK:
        return x
    m = [mm(xi, a - di) for xi, a, di in zip(x, a_list, d)]
    m2 = [mm(mi, mi) for mi in m]
    n = [eye - mi for mi in m]
    n = [ni + mm(ni, p) for ni, p in zip(n, m2)]
    return [mm(ni, xi) for ni, xi in zip(n, x)]


def _resident(shape):
    nd = len(shape)
    return pl.BlockSpec(shape, lambda *_: (0,) * nd, pipeline_mode=pl.Buffered(1))


def _inproj_kernel(x_ref, g_ref, wr_ref, wq_ref, wz_ref, wb_ref, f_ref, q_ref, z_ref, b_ref):
    x = x_ref[...]
    h = x * lax.rsqrt(jnp.mean(x * x, axis=-1, keepdims=True) + RMS_EPS) * g_ref[...]
    hb = h.astype(BF16)
    f_ref[...] = _dot(hb, wr_ref[...])
    q_ref[...] = _dot(hb, wq_ref[...])
    z_ref[...] = _dot(hb, wz_ref[...])
    b_ref[...] = _dot(hb, wb_ref[...])


def _inproj(x2, g, wr, wq, wz, wb, tm):
    n, d = x2.shape
    outs = (R_PROJ, G_QKV, D_G, LANES)
    return pl.pallas_call(
        _inproj_kernel,
        grid=(n // tm,),
        in_specs=[pl.BlockSpec((tm, d), lambda i: (i, 0)), _resident(g.shape),
                  _resident(wr.shape), _resident(wq.shape), _resident(wz.shape), _resident(wb.shape)],
        out_specs=[pl.BlockSpec((tm, w), lambda i: (i, 0)) for w in outs],
        out_shape=[jax.ShapeDtypeStruct((n, w), F32) for w in outs],
        compiler_params=pltpu.CompilerParams(dimension_semantics=("arbitrary",),
                                             vmem_limit_bytes=VMEM_LIMIT_BYTES),
        name="inproj",
    )(x2, g, wr, wq, wz, wb)


def _rwkv_kernel(t, f_ref, shift_ref, s0_ref, mu_ref, w0_ref, wlo_ref, alo_ref, a0_ref, gup_ref,
                 kk_ref, ka_ref, rk_ref, lnw_ref, lnb_ref,
                 out_ref, sout_ref, fbuf, state, ybuf):
    c_idx = pl.program_id(1)
    base = SUBLANES
    hd = R_HEAD_DIM
    pw = R_PAIR * hd

    bb = f_ref.shape[0]

    @pl.when(c_idx == 0)
    def _():
        fbuf[:, base - 1:base, :] = shift_ref[...]
        state[...] = jnp.zeros(state.shape, F32)
        for bi in range(bb):
            for h in range(R_HEADS):
                o = (h % R_PAIR) * hd
                state[bi, h // R_PAIR, o:o + hd, o:o + hd] = s0_ref[bi, h]

    lane = lax.broadcasted_iota(jnp.int32, (t, pw), 1)
    first = lane < hd

    def segsum(*xs):
        outs = []
        for x in xs:
            blocks = []
            for p in range(R_PAIRS):
                xp = x[:, p * pw:(p + 1) * pw]
                s0 = jnp.sum(jnp.where(first, xp, 0.0), axis=-1, keepdims=True)
                s1 = jnp.sum(jnp.where(first, 0.0, xp), axis=-1, keepdims=True)
                blocks.append(jnp.where(first, s0, s1))
            outs.append(jnp.concatenate(blocks, axis=1))
        return outs

    row_t, col_t = _iota2((t, t))
    tri = (row_t >= col_t).astype(BF16)

    def prep(bi):
        f = f_ref[bi]
        fbuf[bi, base:base + t, :] = f
        fp = fbuf[bi, base - 1:base - 1 + t, :]
        fbuf[bi, base - 1:base, :] = f[t - 1:t, :]
        fm = f + (fp - f) * mu_ref[...]

        r = fm[:, 0:D_R]
        k = fm[:, D_R:2 * D_R]
        v = fm[:, 2 * D_R:3 * D_R]
        wa = fm[:, 3 * D_R:3 * D_R + LORA_W + LORA_A]
        gl = fm[:, 3 * D_R + LORA_W + LORA_A:]

        ld = -DECAY_SCALE * _sigmoid(w0_ref[...] + _mm1(jnp.tanh(wa), wlo_ref[...]))
        a = _sigmoid(a0_ref[...] + _mm1(wa, alo_ref[...]))
        g = _mm1(_sigmoid(gl), gup_ref[...])
        kk = k * kk_ref[...]
        keff = k * (1.0 + (a - 1.0) * ka_ref[...])
        kk_ss, bonus_dot = segsum(kk * kk, r * keff * rk_ref[...])
        kk = kk * lax.rsqrt(kk_ss + 1e-12)
        kka = kk * a

        cum = _mm_exact_lhs(tri, ld)
        cl = cum[t - 1:t, :]
        e_mc = jnp.exp(-cum)
        e_lc = jnp.exp(cl - cum)
        stack_t = jnp.concatenate([keff * e_lc, kka * e_lc, jnp.broadcast_to(cl, (SUBLANES, D_R))],
                                  axis=0).T
        return dict(
            rt=(r * jnp.exp(cum)).astype(BF16), kpt=(kk * jnp.exp(cum - ld)).astype(BF16),
            bt=(kka * e_mc).astype(BF16), ktl=(keff * e_mc).astype(BF16), vb=v.astype(BF16),
            kb_t=stack_t[:, :2 * t].astype(BF16),
            pt_col=jnp.exp(stack_t[:, 2 * t:2 * t + 1]),
            gate=g, bonus=bonus_dot * v)

    pre = [prep(bi) for bi in range(bb)]

    row_c, col_c = _iota2((R_PAIR * t, pw))
    m_ch = (row_c // t) == (col_c // hd)
    row_a, col_a = _iota2((R_PAIR * t, R_PAIR * t))
    m_a = (row_a // t) == (col_a // t)
    row_s, col_s = _iota2((pw, pw))
    m_s = (row_s // hd) == (col_s // hd)
    row_g, col_g = _iota2((t, 2 * R_PAIR * t))
    col_g = col_g % t
    strict_g = row_g > col_g
    incl_g = row_g >= col_g
    zero_b = jnp.zeros((), BF16)

    def bd(x, mask):
        return jnp.where(mask, jnp.concatenate([x] * R_PAIR, axis=0), zero_b)

    def mm_pair(x, y):
        return _dot(x.astype(BF16), bd(y.astype(BF16), m_a))

    units = [(bi, p) for bi in range(bb) for p in range(R_PAIRS)]
    n_u = range(len(units))
    pair = lambda name, u: pre[u[0]][name][:, u[1] * pw:(u[1] + 1) * pw]
    v_bd = [bd(pair("vb", u), m_ch) for u in units]
    lhs = [jnp.concatenate([pair("kpt", u), pair("rt", u)], axis=0) for u in units]
    rhs = [jnp.concatenate([bd(pair("bt", u), m_ch), bd(pair("ktl", u), m_ch)], axis=0)
           for u in units]
    gm = [_dot(lhs[i], rhs[i], nt=True) for i in n_u]
    top = [jnp.where(strict_g, gm[i][:t], 0.0) for i in n_u]
    bot = [jnp.where(incl_g, gm[i][t:], 0.0).astype(BF16) for i in n_u]
    av = [_dot(top[i][:, R_PAIR * t:].astype(BF16), v_bd[i]) for i in n_u]
    tinv = _tri_inv([top[i][:, :R_PAIR * t] for i in n_u], mm_pair)
    wu = [_dot(tinv[i].astype(BF16),
               jnp.concatenate([bd(pair("kpt", u), m_ch), bd(av[i].astype(BF16), m_ch)], axis=1))
          for i, u in enumerate(units)]
    s = [state[u[0], u[1]] for u in units]
    ws = [_dot(jnp.concatenate([wu[i][:, :pw].astype(BF16), lhs[i][t:]], axis=0), s[i].astype(BF16))
          for i in n_u]
    zb = [(ws[i][:t] + wu[i][:, pw:]).astype(BF16) for i in n_u]
    yz = [_dot(bot[i], jnp.concatenate([bd(-zb[i], m_ch), v_bd[i]], axis=0)) for i in n_u]
    upd = [_dot(pre[u[0]]["kb_t"][u[1] * pw:(u[1] + 1) * pw],
                jnp.concatenate([pair("vb", u), -zb[i]], axis=0)) for i, u in enumerate(units)]
    for i, (bi, p) in enumerate(units):
        state[bi, p] = s[i] * pre[bi]["pt_col"][p * pw:(p + 1) * pw] + jnp.where(m_s, upd[i], 0.0)
        ybuf[bi, :, p * pw:(p + 1) * pw] = ws[i][t:] + yz[i]

    inv_n = 1.0 / hd
    for bi in range(bb):
        y = ybuf[bi]
        dlt = y - segsum(y)[0] * inv_n
        var = segsum(dlt * dlt)[0] * inv_n
        yn = dlt * lax.rsqrt(var + GN_EPS) * lnw_ref[...] + lnb_ref[...]
        out_ref[bi] = ((yn + pre[bi]["bonus"]) * pre[bi]["gate"]).astype(out_ref.dtype)

    @pl.when(c_idx == pl.num_programs(1) - 1)
    def _():
        for bi in range(bb):
            for h in range(R_HEADS):
                o = (h % R_PAIR) * hd
                sout_ref[bi, h] = state[bi, h // R_PAIR, o:o + hd, o:o + hd]


def _rwkv(f3, shift0, s0_kv, mu, w0, wlo, alo, a0, gup, k_k, k_a, r_k, lnw, lnb):
    b, l, _ = f3.shape
    t = min(CHUNK, l)
    nc = l // t
    bb = RWKV_BATCH_BLOCK if b % RWKV_BATCH_BLOCK == 0 else 1
    row = lambda a: _resident(a.shape)
    return pl.pallas_call(
        functools.partial(_rwkv_kernel, t),
        grid=(b // bb, nc),
        in_specs=[pl.BlockSpec((bb, t, R_PROJ), lambda i, c: (i, c, 0)),
                  pl.BlockSpec((bb, 1, R_PROJ), lambda i, c: (i, 0, 0)),
                  pl.BlockSpec((bb, R_HEADS, R_HEAD_DIM, R_HEAD_DIM), lambda i, c: (i, 0, 0, 0)),
                  row(mu), row(w0), row(wlo), row(alo), row(a0), row(gup),
                  row(k_k), row(k_a), row(r_k), row(lnw), row(lnb)],
        out_specs=[pl.BlockSpec((bb, t, D_R), lambda i, c: (i, c, 0)),
                   pl.BlockSpec((bb, R_HEADS, R_HEAD_DIM, R_HEAD_DIM), lambda i, c: (i, 0, 0, 0))],
        out_shape=[jax.ShapeDtypeStruct((b, l, D_R), BF16),
                   jax.ShapeDtypeStruct((b, R_HEADS, R_HEAD_DIM, R_HEAD_DIM), F32)],
        scratch_shapes=[pltpu.VMEM((bb, SUBLANES + t, R_PROJ), F32),
                        pltpu.VMEM((bb, R_PAIRS, R_PAIR * R_HEAD_DIM, R_PAIR * R_HEAD_DIM), F32),
                        pltpu.VMEM((bb, t, D_R), F32)],
        compiler_params=pltpu.CompilerParams(dimension_semantics=("arbitrary", "arbitrary"),
                                             vmem_limit_bytes=VMEM_LIMIT_BYTES),
        name="rwkv7",
    )(f3, shift0, s0_kv, mu, w0, wlo, alo, a0, gup, k_k, k_a, r_k, lnw, lnb)


def _gdn_kernel(t, qkv_ref, z_ref, ba_ref, conv0_ref, s0_ref, cw_ref, alog_ref, dtb_ref, ng_ref,
                out_ref, sout_ref, cbuf, state):
    c_idx = pl.program_id(1)
    base = SUBLANES
    hist = CONV_W - 1
    bb = qkv_ref.shape[0]

    @pl.when(c_idx == 0)
    def _():
        cbuf[:, base - hist:base, :] = conv0_ref[...]
        state[...] = s0_ref[...]

    row, col = _iota2((t, t))
    tri = (row >= col).astype(BF16)
    mm = _mm1

    units = [(bi, h) for bi in range(bb) for h in range(G_HEADS)]
    qh, kh, vh, bcol, gcol, grow, grow2, glast, kt = {}, {}, {}, {}, {}, {}, {}, {}, {}
    for bi in range(bb):
        qkv = qkv_ref[bi]
        cbuf[bi, base:base + t, :] = qkv
        conv = qkv * cw_ref[hist:hist + 1, :]
        for j in range(hist):
            conv = conv + cbuf[bi, base - hist + j:base - hist + j + t, :] * cw_ref[j:j + 1, :]
        cbuf[bi, base - hist:base, :] = qkv[t - hist:t, :]
        conv = conv * _sigmoid(conv)

        ba = ba_ref[bi]
        beta_t = _sigmoid(ba)
        glog_t = -jnp.exp(alog_ref[...]) * _softplus(ba + dtb_ref[...])
        g_t = _mm_exact_lhs(tri, glog_t)
        g_tt = jnp.concatenate([g_t, g_t], axis=0).T
        k_norm = []
        for h in range(G_HEADS):
            sl = slice(h * G_HEAD_DIM, (h + 1) * G_HEAD_DIM)
            q = conv[:, sl]
            qh[bi, h] = q * lax.rsqrt(jnp.sum(q * q, axis=-1, keepdims=True) + 1e-6) * (G_HEAD_DIM ** -0.5)
            k = conv[:, D_G + h * G_HEAD_DIM:D_G + (h + 1) * G_HEAD_DIM]
            kh[bi, h] = k * lax.rsqrt(jnp.sum(k * k, axis=-1, keepdims=True) + 1e-6)
            k_norm.append(kh[bi, h])
            vh[bi, h] = conv[:, 2 * D_G + h * G_HEAD_DIM:2 * D_G + (h + 1) * G_HEAD_DIM]
            bcol[bi, h] = beta_t[:, h:h + 1]
            gcol[bi, h] = g_t[:, G_HEADS + h:G_HEADS + h + 1]
            grow2[bi, h] = g_tt[G_HEADS + h:G_HEADS + h + 1, :]
            grow[bi, h] = grow2[bi, h][:, :t]
            glast[bi, h] = gcol[bi, h][t - 1:t, :]
        kt_all = jnp.concatenate(k_norm, axis=1).T
        for h in range(G_HEADS):
            kt[bi, h] = kt_all[h * G_HEAD_DIM:(h + 1) * G_HEAD_DIM]

    hdim = G_HEAD_DIM
    pairs = [(bi, pp) for bi in range(bb) for pp in range(G_HEADS // 2)]
    n_p = range(len(pairs))
    heads_of = lambda pr: ((pr[0], 2 * pr[1]), (pr[0], 2 * pr[1] + 1))
    row_k, col_k = _iota2((2 * t, 2 * hdim))
    m_k = (row_k // t) == (col_k // hdim)
    row_a, col_a = _iota2((2 * t, 2 * t))
    m_a = (row_a // t) == (col_a // t)
    top_rows = row_k < t
    row_p, col_p = _iota2((t, 2 * t))
    first = col_p < t
    col_p = col_p % t
    incl_p = row_p >= col_p
    strict_p = row_p > col_p
    zero_b = jnp.zeros((), BF16)

    def mm_pair(x, y):
        yb = y.astype(BF16)
        return _dot(x.astype(BF16), jnp.where(m_a, jnp.concatenate([yb, yb], axis=0), zero_b))

    def bd_cols(x0, x1):
        xs = jnp.concatenate([x0, x1], axis=0)
        return jnp.concatenate([jnp.where(top_rows[:, :x0.shape[1]], xs, zero_b),
                                jnp.where(top_rows[:, :x0.shape[1]], zero_b, xs)], axis=1)

    kb, eg = {}, {}
    for u_ in units:
        kb[u_] = kh[u_] * bcol[u_]
        eg[u_] = jnp.exp(gcol[u_])
    dec_incl, lhs, rhs = [], [], []
    for pr in pairs:
        h0, h1 = heads_of(pr)
        dg = jnp.where(first, gcol[h0] - grow2[h0], gcol[h1] - grow2[h1])
        dec_incl.append(jnp.where(incl_p, jnp.exp(jnp.where(incl_p, dg, 0.0)), 0.0))
        lhs.append(jnp.concatenate([jnp.concatenate([kb[h0], kb[h1]], axis=1),
                                    jnp.concatenate([qh[h0], qh[h1]], axis=1)], axis=0).astype(BF16))
        khp = jnp.concatenate([kh[h0], kh[h1]], axis=1).astype(BF16)
        rhs.append(jnp.where(m_k, jnp.concatenate([khp, khp], axis=0), zero_b))
    gm = [_dot(lhs[i], rhs[i], nt=True) for i in n_p]
    a_mat = [gm[i][:t] * jnp.where(strict_p, dec_incl[i], 0.0) for i in n_p]
    qk = [(gm[i][t:] * dec_incl[i]).astype(BF16) for i in n_p]
    tinv = _tri_inv(a_mat, mm_pair)
    uw_p = []
    for i, pr in enumerate(pairs):
        x0, x1 = [jnp.concatenate([vh[u_] * bcol[u_], kb[u_] * eg[u_]], axis=1).astype(BF16)
                  for u_ in heads_of(pr)]
        uw_p.append(_dot(tinv[i].astype(BF16), bd_cols(x0, x1)))
    uw = [uw_p[i // 2][:, (i % 2) * 2 * hdim:(i % 2 + 1) * 2 * hdim] for i in range(len(units))]
    s = [state[u_[0], u_[1]] for u_ in units]
    wqs = [mm(jnp.concatenate([uw[i][:, hdim:], qh[u_] * eg[u_]], axis=0), s[i])
           for i, u_ in enumerate(units)]
    v_new = [uw[i][:, :hdim] - wqs[i][:t] for i in range(len(units))]
    qkv_p = [_dot(qk[i], bd_cols(v_new[2 * i].astype(BF16), v_new[2 * i + 1].astype(BF16))) for i in n_p]
    qkv_new = [qkv_p[i // 2][:, (i % 2) * hdim:(i % 2 + 1) * hdim] for i in range(len(units))]
    kdv = [mm(kt[u_] * jnp.exp(glast[u_] - grow[u_]), v_new[i]) for i, u_ in enumerate(units)]
    for i, (bi, h) in enumerate(units):
        sl = slice(h * G_HEAD_DIM, (h + 1) * G_HEAD_DIM)
        state[bi, h] = s[i] * jnp.exp(glast[bi, h]) + kdv[i]
        o = wqs[i][t:] + qkv_new[i]
        zh = z_ref[bi, :, sl]
        o = (o * lax.rsqrt(jnp.mean(o * o, axis=-1, keepdims=True) + RMS_EPS) * ng_ref[...]
             * (zh * _sigmoid(zh)))
        out_ref[bi, :, sl] = o.astype(out_ref.dtype)

    @pl.when(c_idx == pl.num_programs(1) - 1)
    def _():
        sout_ref[...] = state[...]


def _gdn(qkv3, z3, ba3, conv0, s0, cw, alog, dtb, ng):
    b, l, _ = qkv3.shape
    t = min(CHUNK, l)
    nc = l // t
    bb = GDN_BATCH_BLOCK if b % GDN_BATCH_BLOCK == 0 else 1
    row = lambda a: _resident(a.shape)
    return pl.pallas_call(
        functools.partial(_gdn_kernel, t),
        grid=(b // bb, nc),
        in_specs=[pl.BlockSpec((bb, t, G_QKV), lambda i, c: (i, c, 0)),
                  pl.BlockSpec((bb, t, D_G), lambda i, c: (i, c, 0)),
                  pl.BlockSpec((bb, t, LANES), lambda i, c: (i, c, 0)),
                  pl.BlockSpec((bb, CONV_W - 1, G_QKV), lambda i, c: (i, 0, 0)),
                  pl.BlockSpec((bb, G_HEADS, G_HEAD_DIM, G_HEAD_DIM), lambda i, c: (i, 0, 0, 0)),
                  row(cw), row(alog), row(dtb), row(ng)],
        out_specs=[pl.BlockSpec((bb, t, D_G), lambda i, c: (i, c, 0)),
                   pl.BlockSpec((bb, G_HEADS, G_HEAD_DIM, G_HEAD_DIM), lambda i, c: (i, 0, 0, 0))],
        out_shape=[jax.ShapeDtypeStruct((b, l, D_G), BF16),
                   jax.ShapeDtypeStruct((b, G_HEADS, G_HEAD_DIM, G_HEAD_DIM), F32)],
        scratch_shapes=[pltpu.VMEM((bb, SUBLANES + t, G_QKV), F32),
                        pltpu.VMEM((bb, G_HEADS, G_HEAD_DIM, G_HEAD_DIM), F32)],
        compiler_params=pltpu.CompilerParams(dimension_semantics=("arbitrary", "arbitrary"),
                                             vmem_limit_bytes=VMEM_LIMIT_BYTES),
        name="gdn",
    )(qkv3, z3, ba3, conv0, s0, cw, alog, dtb, ng)


def _rms(x, g):
    return x * lax.rsqrt(jnp.mean(x * x, axis=-1, keepdims=True) + RMS_EPS) * g


def _tail_kernel(x_ref, r_ref, g_ref, p_ref, wor_ref, wog_ref, lnf_ref, wgate_ref, wup_ref, wdown_ref,
                 lnp_ref, wpg_ref, wpp_ref, fng_ref, y_ref):
    x = x_ref[...] + _dot(r_ref[...], wor_ref[...]) + _dot(g_ref[...], wog_ref[...])
    hb = _rms(x, lnf_ref[...]).astype(BF16)
    d_ff = wgate_ref.shape[1]
    ffn = None
    for j in range(d_ff // FF_TILE):
        cs = slice(j * FF_TILE, (j + 1) * FF_TILE)
        gate = _dot(hb, wgate_ref[:, cs])
        up = _dot(hb, wup_ref[:, cs])
        act = (gate * _sigmoid(gate)) * up
        down = _dot(act.astype(BF16), wdown_ref[cs, :])
        ffn = down if ffn is None else ffn + down
    x = x + ffn
    hb = _rms(x, lnp_ref[...]).astype(BF16)
    x = x + _sigmoid(_dot(hb, wpg_ref[...])) * _dot(p_ref[...].astype(BF16), wpp_ref[...])
    y_ref[...] = _rms(x, fng_ref[...])


def _tail(x2, r2, g2, p2, wor, wog, lnf, wgate, wup, wdown, lnp, wpg, wpp, fng, tm):
    n, d = x2.shape
    tok = lambda w: pl.BlockSpec((tm, w), lambda i: (i, 0))
    consts = (wor, wog, lnf, wgate, wup, wdown, lnp, wpg, wpp, fng)
    return pl.pallas_call(
        _tail_kernel,
        grid=(n // tm,),
        in_specs=[tok(d), tok(r2.shape[1]), tok(g2.shape[1]), tok(p2.shape[1])]
                 + [_resident(a.shape) for a in consts],
        out_specs=tok(d),
        out_shape=jax.ShapeDtypeStruct((n, d), F32),
        compiler_params=pltpu.CompilerParams(dimension_semantics=("arbitrary",),
                                             vmem_limit_bytes=VMEM_LIMIT_BYTES),
        name="tail",
    )(x2, r2, g2, p2, *consts)


def _pad_lanes(a, lanes, offset=0):
    out = jnp.zeros(a.shape[:-1] + (lanes,), a.dtype)
    return lax.dynamic_update_slice_in_dim(out, a, offset, axis=a.ndim - 1)


def _layer(x, p, shift0, wkv0, conv0, gdn0, wts, final_g):
    (ln_mix_g, w_in, mu_shift, w0, w_lora_up, a0, a_lora_up, g_lora_up, k_k, k_a, r_k, ln_x_w, ln_x_b,
     conv_w, a_log, dt_bias, gdn_norm_g, w_out, ln_ffn_g, w_gate, w_up, w_down, ln_ple_g,
     w_ple_gate, w_ple_proj) = wts
    b, l, d = x.shape
    n = b * l
    tm = min(TOKEN_TILE, n)
    row = lambda a: a.reshape(1, -1).astype(F32)

    off_z = R_PROJ + G_QKV
    off_b = off_z + D_G
    w_in_b = w_in.astype(BF16)
    wr, wq, wz = w_in_b[:, :R_PROJ], w_in_b[:, R_PROJ:off_z], w_in_b[:, off_z:off_b]
    wb = _pad_lanes(w_in_b[:, off_b:], LANES)
    x2 = x.reshape(n, d)
    f2, qkv2, z2, ba2 = _inproj(x2, row(ln_mix_g), wr, wq, wz, wb, tm)

    zeros_lora = jnp.zeros((LORA_W, D_R), BF16)
    wlo = jnp.concatenate([w_lora_up.astype(BF16), zeros_lora], axis=0)
    alo = jnp.concatenate([zeros_lora, a_lora_up.astype(BF16)], axis=0)
    f3 = f2.reshape(b, l, R_PROJ)
    r_out, wkv1_kv = _rwkv(f3, shift0.reshape(b, 1, R_PROJ), jnp.swapaxes(wkv0, -1, -2),
                           row(mu_shift), row(w0), wlo, alo, row(a0), g_lora_up.astype(BF16),
                           row(k_k), row(k_a), row(r_k), row(ln_x_w), row(ln_x_b))
    wkv1 = jnp.swapaxes(wkv1_kv, -1, -2)
    shift1 = f3[:, -1]

    qkv3 = qkv2.reshape(b, l, G_QKV)
    g_out, gdn1 = _gdn(qkv3, z2.reshape(b, l, D_G), ba2.reshape(b, l, LANES), conv0, gdn0,
                       conv_w, _pad_lanes(row(a_log), LANES, G_HEADS),
                       _pad_lanes(row(dt_bias), LANES, G_HEADS), row(gdn_norm_g))
    conv1 = jnp.concatenate([conv0, qkv3], axis=1)[:, l:]

    w_out_b = w_out.astype(BF16)
    y2 = _tail(x2, r_out.reshape(n, D_R), g_out.reshape(n, D_G), p.reshape(n, -1),
               w_out_b[:D_R], w_out_b[D_R:], row(ln_ffn_g), w_gate.astype(BF16), w_up.astype(BF16),
               w_down.astype(BF16), row(ln_ple_g), w_ple_gate.astype(BF16), w_ple_proj.astype(BF16),
               row(final_g), tm)
    return y2.reshape(b, l, d), shift1, wkv1, conv1, gdn1


def kernel(x_prompt, x_sample, state_shift, state_wkv, state_conv, state_gdn, p_prompt, p_sample,
           ln_mix_g, w_in, mu_shift, w0, w_lora_up, a0, a_lora_up, g_lora_up, k_k, k_a, r_k,
           ln_x_w, ln_x_b, conv_w, a_log, dt_bias, gdn_norm_g, w_out, ln_ffn_g, w_gate, w_up,
           w_down, ln_ple_g, w_ple_gate, w_ple_proj, final_norm_g):
    depth = w_in.shape[0]
    assert depth == 1, "the final norm is fused into the single layer's tail kernel"
    bp = x_prompt.shape[0]
    dt = x_prompt.dtype
    wts = tuple(a[0] for a in (ln_mix_g, w_in, mu_shift, w0, w_lora_up, a0, a_lora_up, g_lora_up, k_k,
                               k_a, r_k, ln_x_w, ln_x_b, conv_w, a_log, dt_bias, gdn_norm_g, w_out,
                               ln_ffn_g, w_gate, w_up, w_down, ln_ple_g, w_ple_gate, w_ple_proj))
    z_shift = jnp.zeros((bp, R_PROJ), dt)
    z_wkv = jnp.zeros((bp, R_HEADS, R_HEAD_DIM, R_HEAD_DIM), dt)
    z_conv = jnp.zeros((bp, CONV_W - 1, G_QKV), dt)
    z_gdn = jnp.zeros((bp, G_HEADS, G_HEAD_DIM, G_HEAD_DIM), dt)
    yp, a1, a2, a3, a4 = _layer(x_prompt, p_prompt[0], z_shift, z_wkv, z_conv, z_gdn, wts, final_norm_g)
    ys, b1, b2, b3, b4 = _layer(x_sample, p_sample[0], state_shift[0], state_wkv[0], state_conv[0],
                                state_gdn[0], wts, final_norm_g)
    stack = lambda a: a[None]
    return (yp, ys, stack(a1), stack(a2), stack(a3), stack(a4),
            stack(b1), stack(b2), stack(b3), stack(b4))
```

```python
import functools
import itertools
import math

import jax
import jax.numpy as jnp
from jax import lax
from jax.experimental import pallas as pl
from jax.experimental.pallas import tpu as pltpu

F32 = jnp.float32
BF16 = jnp.bfloat16

R_HEADS = 8
R_HEAD_DIM = 64
D_R = R_HEADS * R_HEAD_DIM
LORA_W = 64
LORA_A = 64
LORA_G = 128
R_PROJ = 3 * D_R + LORA_W + LORA_A + LORA_G
G_HEADS = 4
G_HEAD_DIM = 128
D_G = G_HEADS * G_HEAD_DIM
CONV_W = 4
G_QKV = 3 * D_G
CHUNK = 64
RMS_EPS = 1e-6
GN_EPS = 64e-5
DECAY_SCALE = math.exp(-0.5)

LANES = 128
SUBLANES = 8
VMEM_LIMIT_BYTES = 60 * 1024 * 1024
INV_BLOCK = 16
FF_TILE = 256
BATCH_BLOCK = 4
R_PAIR = LANES // R_HEAD_DIM
R_PAIRS = R_HEADS // R_PAIR


def _dot(a, b, nt=False):
    dn = (((1,), (1,)), ((), ())) if nt else (((1,), (0,)), ((), ()))
    return lax.dot_general(a, b, dn, preferred_element_type=F32)


def _mm1(a, b, nt=False):
    return _dot(a.astype(BF16), b.astype(BF16), nt)


def _mm_exact_lhs(a_bf16, b):
    b1 = b.astype(BF16)
    r1 = b - b1.astype(F32)
    b2 = r1.astype(BF16)
    b3 = (r1 - b2.astype(F32)).astype(BF16)
    return _dot(a_bf16, b1) + (_dot(a_bf16, b2) + _dot(a_bf16, b3))


def _sigmoid(x):
    return 0.5 * jnp.tanh(0.5 * x) + 0.5


def _softplus(x):
    return jnp.maximum(x, 0.0) + jnp.log1p(jnp.exp(-jnp.abs(x)))


def _rms(x, g):
    return x * lax.rsqrt(jnp.mean(x * x, axis=-1, keepdims=True) + RMS_EPS) * g


def _iota2(shape):
    return (lax.broadcasted_iota(jnp.int32, shape, 0), lax.broadcasted_iota(jnp.int32, shape, 1))


def _tri_inv(a_list, mm, tick):
    t, width = a_list[0].shape
    assert t <= 4 * INV_BLOCK and width % t == 0
    row, col = _iota2((t, width))
    col = col % t
    eye = (row == col).astype(F32)
    if t > INV_BLOCK:
        blk = (row // INV_BLOCK) == (col // INV_BLOCK)
        d = [jnp.where(blk, a, 0.0) for a in a_list]
    else:
        d = a_list
    d2 = [mm(di, di) for di in d]
    tick()
    x = [eye - di for di in d]
    d4 = [mm(p, p) for p in d2]
    tick()
    x = [xi + mm(xi, p) for xi, p in zip(x, d2)]
    tick()
    d8 = [mm(p, p) for p in d4]
    tick()
    x = [xi + mm(xi, p) for xi, p in zip(x, d4)]
    tick()
    x = [xi + mm(xi, p) for xi, p in zip(x, d8)]
    tick()
    if t <= INV_BLOCK:
        return x
    m = [mm(xi, a - di) for xi, a, di in zip(x, a_list, d)]
    tick()
    m2 = [mm(mi, mi) for mi in m]
    tick()
    n = [eye - mi for mi in m]
    n = [ni + mm(ni, p) for ni, p in zip(n, m2)]
    tick()
    return [mm(ni, xi) for ni, xi in zip(n, x)]


def _pair_mm(t):
    row, col = _iota2((2 * t, 2 * t))
    mask = (row // t) == (col // t)
    zero = jnp.zeros((), BF16)

    def mm(x, y):
        yb = y.astype(BF16)
        return _dot(x.astype(BF16), jnp.where(mask, jnp.concatenate([yb, yb], axis=0), zero))
    return mm


def _rwkv_mixer(t, bb, f_rows, w, fbuf, state, ybuf, rg, tick):
    base = SUBLANES
    hd = R_HEAD_DIM
    pw = R_PAIR * hd

    lane = lax.broadcasted_iota(jnp.int32, (t, pw), 1)
    first = lane < hd

    def segsum(*xs):
        outs = []
        for x in xs:
            blocks = []
            for p in range(R_PAIRS):
                xp = x[:, p * pw:(p + 1) * pw]
                s0 = jnp.sum(jnp.where(first, xp, 0.0), axis=-1, keepdims=True)
                s1 = jnp.sum(jnp.where(first, 0.0, xp), axis=-1, keepdims=True)
                blocks.append(jnp.where(first, s0, s1))
            outs.append(jnp.concatenate(blocks, axis=1))
        return outs

    row_t, col_t = _iota2((t, t))
    tri = (row_t >= col_t).astype(BF16)

    def prep(bi):
        f = f_rows[bi]
        fbuf[bi, base:base + t, :] = f
        fp = fbuf[bi, base - 1:base - 1 + t, :]
        fbuf[bi, base - 1:base, :] = f[t - 1:t, :]
        fm = f + (fp - f) * w.mu[...]

        r = fm[:, 0:D_R]
        k = fm[:, D_R:2 * D_R]
        v = fm[:, 2 * D_R:3 * D_R]
        wa = fm[:, 3 * D_R:3 * D_R + LORA_W + LORA_A]
        gl = fm[:, 3 * D_R + LORA_W + LORA_A:]

        ld = -DECAY_SCALE * _sigmoid(w.w0[...] + _mm1(jnp.tanh(wa), w.wlo[...]))
        a = _sigmoid(w.a0[...] + _mm1(wa, w.alo[...]))
        g = _mm1(_sigmoid(gl), w.gup[...])
        kk = k * w.kk[...]
        keff = k * (1.0 + (a - 1.0) * w.ka[...])
        kk_ss, bonus_dot = segsum(kk * kk, r * keff * w.rk[...])
        kk = kk * lax.rsqrt(kk_ss + 1e-12)
        kka = kk * a

        cum = _mm_exact_lhs(tri, ld)
        cl = cum[t - 1:t, :]
        e_mc = jnp.exp(-cum)
        e_lc = jnp.exp(cl - cum)
        stack_t = jnp.concatenate([keff * e_lc, kka * e_lc, jnp.broadcast_to(cl, (SUBLANES, D_R))],
                                  axis=0).T
        tick()
        return dict(
            rt=(r * jnp.exp(cum)).astype(BF16), kpt=(kk * jnp.exp(cum - ld)).astype(BF16),
            bt=(kka * e_mc).astype(BF16), ktl=(keff * e_mc).astype(BF16), vb=v.astype(BF16),
            kb_t=stack_t[:, :2 * t].astype(BF16),
            pt_col=jnp.exp(stack_t[:, 2 * t:2 * t + 1]),
            gate=g, bonus=bonus_dot * v)

    pre = [prep(bi) for bi in range(bb)]

    row_c, col_c = _iota2((R_PAIR * t, pw))
    m_ch = (row_c // t) == (col_c // hd)
    row_s, col_s = _iota2((pw, pw))
    m_s = (row_s // hd) == (col_s // hd)
    row_g, col_g = _iota2((t, 2 * R_PAIR * t))
    col_g = col_g % t
    strict_g = row_g > col_g
    incl_g = row_g >= col_g
    zero_b = jnp.zeros((), BF16)
    mm_pair = _pair_mm(t)

    def bd(x, mask):
        return jnp.where(mask, jnp.concatenate([x] * R_PAIR, axis=0), zero_b)

    units = [(bi, p) for bi in range(bb) for p in range(R_PAIRS)]
    n_u = range(len(units))
    pair = lambda name, u: pre[u[0]][name][:, u[1] * pw:(u[1] + 1) * pw]
    v_bd = [bd(pair("vb", u), m_ch) for u in units]
    lhs = [jnp.concatenate([pair("kpt", u), pair("rt", u)], axis=0) for u in units]
    rhs = [jnp.concatenate([bd(pair("bt", u), m_ch), bd(pair("ktl", u), m_ch)], axis=0)
           for u in units]
    gm = [_dot(lhs[i], rhs[i], nt=True) for i in n_u]
    tick()
    top = [jnp.where(strict_g, gm[i][:t], 0.0) for i in n_u]
    bot = [jnp.where(incl_g, gm[i][t:], 0.0).astype(BF16) for i in n_u]
    av = [_dot(top[i][:, R_PAIR * t:].astype(BF16), v_bd[i]) for i in n_u]
    tick()
    tinv = _tri_inv([top[i][:, :R_PAIR * t] for i in n_u], mm_pair, tick)
    tick()
    wu = [_dot(tinv[i].astype(BF16),
               jnp.concatenate([bd(pair("kpt", u), m_ch), bd(av[i].astype(BF16), m_ch)], axis=1))
          for i, u in enumerate(units)]
    tick()
    s = [state[u[0], u[1]] for u in units]
    ws = [_dot(jnp.concatenate([wu[i][:, :pw].astype(BF16), lhs[i][t:]], axis=0), s[i].astype(BF16))
          for i in n_u]
    tick()
    zb = [(ws[i][:t] + wu[i][:, pw:]).astype(BF16) for i in n_u]
    yz = [_dot(bot[i], jnp.concatenate([bd(-zb[i], m_ch), v_bd[i]], axis=0)) for i in n_u]
    upd = [_dot(pre[u[0]]["kb_t"][u[1] * pw:(u[1] + 1) * pw],
                jnp.concatenate([pair("vb", u), -zb[i]], axis=0)) for i, u in enumerate(units)]
    tick()
    for i, (bi, p) in enumerate(units):
        state[bi, p] = s[i] * pre[bi]["pt_col"][p * pw:(p + 1) * pw] + jnp.where(m_s, upd[i], 0.0)
        ybuf[bi, :, p * pw:(p + 1) * pw] = ws[i][t:] + yz[i]

    inv_n = 1.0 / hd
    for bi in range(bb):
        y = ybuf[bi]
        dlt = y - segsum(y)[0] * inv_n
        var = segsum(dlt * dlt)[0] * inv_n
        yn = dlt * lax.rsqrt(var + GN_EPS) * w.lnw[...] + w.lnb[...]
        rg[bi * t:(bi + 1) * t, 0:D_R] = ((yn + pre[bi]["bonus"]) * pre[bi]["gate"]).astype(rg.dtype)


def _gdn_mixer(t, bb, qkv_rows, z_rows, ba_rows, w, cbuf, state, rg, tick):
    base = SUBLANES
    hist = CONV_W - 1
    hdim = G_HEAD_DIM

    row, col = _iota2((t, t))
    tri = (row >= col).astype(BF16)
    mm = _mm1

    units = [(bi, h) for bi in range(bb) for h in range(G_HEADS)]
    qh, kh, vh, bcol, gcol, grow, grow2, glast, kt = {}, {}, {}, {}, {}, {}, {}, {}, {}
    for bi in range(bb):
        qkv = qkv_rows[bi]
        cbuf[bi, base:base + t, :] = qkv
        conv = qkv * w.cw[hist:hist + 1, :]
        for j in range(hist):
            conv = conv + cbuf[bi, base - hist + j:base - hist + j + t, :] * w.cw[j:j + 1, :]
        cbuf[bi, base - hist:base, :] = qkv[t - hist:t, :]
        conv = conv * _sigmoid(conv)

        ba = ba_rows[bi]
        beta_t = _sigmoid(ba)
        glog_t = -jnp.exp(w.alog[...]) * _softplus(ba + w.dtb[...])
        g_t = _mm_exact_lhs(tri, glog_t)
        g_tt = jnp.concatenate([g_t, g_t], axis=0).T
        k_norm = []
        for h in range(G_HEADS):
            sl = slice(h * hdim, (h + 1) * hdim)
            q = conv[:, sl]
            qh[bi, h] = q * lax.rsqrt(jnp.sum(q * q, axis=-1, keepdims=True) + 1e-6) * (hdim ** -0.5)
            k = conv[:, D_G + h * hdim:D_G + (h + 1) * hdim]
            kh[bi, h] = k * lax.rsqrt(jnp.sum(k * k, axis=-1, keepdims=True) + 1e-6)
            k_norm.append(kh[bi, h])
            vh[bi, h] = conv[:, 2 * D_G + h * hdim:2 * D_G + (h + 1) * hdim]
            bcol[bi, h] = beta_t[:, h:h + 1]
            gcol[bi, h] = g_t[:, G_HEADS + h:G_HEADS + h + 1]
            grow2[bi, h] = g_tt[G_HEADS + h:G_HEADS + h + 1, :]
            grow[bi, h] = grow2[bi, h][:, :t]
            glast[bi, h] = gcol[bi, h][t - 1:t, :]
        kt_all = jnp.concatenate(k_norm, axis=1).T
        for h in range(G_HEADS):
            kt[bi, h] = kt_all[h * hdim:(h + 1) * hdim]
        tick()

    pairs = [(bi, pp) for bi in range(bb) for pp in range(G_HEADS // 2)]
    n_p = range(len(pairs))
    heads_of = lambda pr: ((pr[0], 2 * pr[1]), (pr[0], 2 * pr[1] + 1))
    row_k, col_k = _iota2((2 * t, 2 * hdim))
    m_k = (row_k // t) == (col_k // hdim)
    top_rows = row_k < t
    row_p, col_p = _iota2((t, 2 * t))
    first = col_p < t
    col_p = col_p % t
    incl_p = row_p >= col_p
    strict_p = row_p > col_p
    zero_b = jnp.zeros((), BF16)
    mm_pair = _pair_mm(t)

    def bd_cols(x0, x1):
        xs = jnp.concatenate([x0, x1], axis=0)
        return jnp.concatenate([jnp.where(top_rows[:, :x0.shape[1]], xs, zero_b),
                                jnp.where(top_rows[:, :x0.shape[1]], zero_b, xs)], axis=1)

    kb, eg = {}, {}
    for u_ in units:
        kb[u_] = kh[u_] * bcol[u_]
        eg[u_] = jnp.exp(gcol[u_])
    dec_incl, lhs, rhs = [], [], []
    for pr in pairs:
        h0, h1 = heads_of(pr)
        dg = jnp.where(first, gcol[h0] - grow2[h0], gcol[h1] - grow2[h1])
        dec_incl.append(jnp.where(incl_p, jnp.exp(jnp.where(incl_p, dg, 0.0)), 0.0))
        lhs.append(jnp.concatenate([jnp.concatenate([kb[h0], kb[h1]], axis=1),
                                    jnp.concatenate([qh[h0], qh[h1]], axis=1)], axis=0).astype(BF16))
        khp = jnp.concatenate([kh[h0], kh[h1]], axis=1).astype(BF16)
        rhs.append(jnp.where(m_k, jnp.concatenate([khp, khp], axis=0), zero_b))
    gm = [_dot(lhs[i], rhs[i], nt=True) for i in n_p]
    a_mat = [gm[i][:t] * jnp.where(strict_p, dec_incl[i], 0.0) for i in n_p]
    qk = [(gm[i][t:] * dec_incl[i]).astype(BF16) for i in n_p]
    tick()
    tinv = _tri_inv(a_mat, mm_pair, tick)
    tick()
    uw_p = []
    for i, pr in enumerate(pairs):
        x0, x1 = [jnp.concatenate([vh[u_] * bcol[u_], kb[u_] * eg[u_]], axis=1).astype(BF16)
                  for u_ in heads_of(pr)]
        uw_p.append(_dot(tinv[i].astype(BF16), bd_cols(x0, x1)))
    tick()
    uw = [uw_p[i // 2][:, (i % 2) * 2 * hdim:(i % 2 + 1) * 2 * hdim] for i in range(len(units))]
    s = [state[u_[0], u_[1]] for u_ in units]
    wqs = [mm(jnp.concatenate([uw[i][:, hdim:], qh[u_] * eg[u_]], axis=0), s[i])
           for i, u_ in enumerate(units)]
    tick()
    v_new = [uw[i][:, :hdim] - wqs[i][:t] for i in range(len(units))]
    qkv_p = [_dot(qk[i], bd_cols(v_new[2 * i].astype(BF16), v_new[2 * i + 1].astype(BF16))) for i in n_p]
    qkv_new = [qkv_p[i // 2][:, (i % 2) * hdim:(i % 2 + 1) * hdim] for i in range(len(units))]
    kdv = [mm(kt[u_] * jnp.exp(glast[u_] - grow[u_]), v_new[i]) for i, u_ in enumerate(units)]
    tick()
    for i, (bi, h) in enumerate(units):
        sl = slice(h * hdim, (h + 1) * hdim)
        state[bi, h] = s[i] * jnp.exp(glast[bi, h]) + kdv[i]
        o = wqs[i][t:] + qkv_new[i]
        zh = z_rows[bi][:, sl]
        o = (o * lax.rsqrt(jnp.mean(o * o, axis=-1, keepdims=True) + RMS_EPS) * w.ng[...]
             * (zh * _sigmoid(zh)))
        rg[bi * t:(bi + 1) * t, D_R + h * hdim:D_R + (h + 1) * hdim] = o.astype(rg.dtype)


def _tail_steps(x, p, w, rg, y_ref, bb, t):
    x = x + _dot(rg[...], w.wo[...])
    yield
    hb = _rms(x, w.lnf[...]).astype(BF16)
    d_ff = w.wgate.shape[1]
    ffn = None
    for j in range(d_ff // FF_TILE):
        cs = slice(j * FF_TILE, (j + 1) * FF_TILE)
        gate = _dot(hb, w.wgate[:, cs])
        yield
        up = _dot(hb, w.wup[:, cs])
        yield
        act = (gate * _sigmoid(gate)) * up
        down = _dot(act.astype(BF16), w.wdown[cs, :])
        ffn = down if ffn is None else ffn + down
        yield
    x = x + ffn
    hb = _rms(x, w.lnp[...]).astype(BF16)
    ple_gate = _sigmoid(_dot(hb, w.wpg[...]))
    yield
    x = x + ple_gate * _dot(p.astype(BF16), w.wpp[...])
    y = _rms(x, w.fng[...])
    for bi in range(bb):
        y_ref[bi] = y[bi * t:(bi + 1) * t]
    yield


_WEIGHT_NAMES = ("lnm", "wr", "wq", "wz", "wb",
                 "mu", "w0", "wlo", "alo", "a0", "gup", "kk", "ka", "rk", "lnw", "lnb",
                 "cw", "alog", "dtb", "ng",
                 "wo", "lnf", "wgate", "wup", "wdown", "lnp", "wpg", "wpp", "fng")


class _Refs:
    def __init__(self, names, refs):
        for n, r in zip(names, refs):
            setattr(self, n, r)


def _layer_kernel(t, n_chunks, lag, x_ref, xl_ref, pl_ref, shift_ref, wkv0_ref, conv0_ref, gdn0_ref,
                  *rest):
    nw = len(_WEIGHT_NAMES)
    w = _Refs(_WEIGHT_NAMES, rest[:nw])
    y_ref, shift_out, wkv_out, conv_out, gdn_out = rest[nw:nw + 5]
    fbuf, cbuf, rstate, gstate, ybuf, rg = rest[nw + 5:]
    c_idx = pl.program_id(1)
    bb = x_ref.shape[0]
    base = SUBLANES
    hist = CONV_W - 1
    hd = R_HEAD_DIM

    @pl.when(c_idx == 0)
    def _():
        fbuf[:, base - 1:base, :] = shift_ref[...]
        cbuf[:, base - hist:base, :] = conv0_ref[...]
        gstate[...] = gdn0_ref[...]
        rstate[...] = jnp.zeros(rstate.shape, F32)
        for bi in range(bb):
            for h in range(R_HEADS):
                o = (h % R_PAIR) * hd
                rstate[bi, h // R_PAIR, o:o + hd, o:o + hd] = wkv0_ref[bi, h]
        rg[...] = jnp.zeros(rg.shape, rg.dtype)

    rows = lambda ref: jnp.concatenate([ref[bi] for bi in range(bb)], axis=0)
    tail = _tail_steps(rows(xl_ref), rows(pl_ref), w, rg, y_ref, bb, t)

    hb = _rms(rows(x_ref), w.lnm[...]).astype(BF16)
    split = lambda a: [a[bi * t:(bi + 1) * t] for bi in range(bb)]
    f_rows = split(_dot(hb, w.wr[...]))
    proj = {}

    def gdn_inproj():
        for name, ref in (("qkv", w.wq), ("z", w.wz), ("ba", w.wb)):
            proj[name] = split(_dot(hb, ref[...]))
            yield

    late = gdn_inproj()
    filler = itertools.chain(late, tail) if lag else late

    def tick():
        next(filler, None)

    if lag:
        next(tail)
    _rwkv_mixer(t, bb, f_rows, w, fbuf, rstate, ybuf, rg, tick)
    for _ in late:
        pass
    qkv_rows = proj["qkv"]
    _gdn_mixer(t, bb, qkv_rows, proj["z"], proj["ba"], w, cbuf, gstate, rg, tick)
    for _ in tail:
        pass

    @pl.when(c_idx == n_chunks - 1)
    def _():
        for bi in range(bb):
            shift_out[bi] = f_rows[bi][t - 1:t, :]
            conv_out[bi] = qkv_rows[bi][t - hist:t, :]
            for h in range(R_HEADS):
                o = (h % R_PAIR) * hd
                wkv_out[bi, h] = rstate[bi, h // R_PAIR, o:o + hd, o:o + hd]
        gdn_out[...] = gstate[...]


def _resident(a):
    nd = a.ndim
    return pl.BlockSpec(a.shape, lambda *_: (0,) * nd, pipeline_mode=pl.Buffered(1))


def _layer_call(x, p, shift0, wkv0_kv, conv0, gdn0, weights):
    b, l, d = x.shape
    t = min(CHUNK, l)
    nc = l // t
    lag = 1 if nc > 1 else 0
    bb = BATCH_BLOCK if b % BATCH_BLOCK == 0 else 1
    cur = lambda i, c: (i, jnp.minimum(c, nc - 1), 0)
    old = lambda i, c: (i, jnp.maximum(c - lag, 0), 0)
    per_stream = lambda a: pl.BlockSpec((bb,) + a.shape[1:], lambda i, c: (i,) + (0,) * (a.ndim - 1))
    state_shapes = (shift0.shape, wkv0_kv.shape, conv0.shape, gdn0.shape)
    return pl.pallas_call(
        functools.partial(_layer_kernel, t, nc, lag),
        grid=(b // bb, nc + lag),
        in_specs=[pl.BlockSpec((bb, t, d), cur), pl.BlockSpec((bb, t, d), old),
                  pl.BlockSpec((bb, t, p.shape[2]), old),
                  per_stream(shift0), per_stream(wkv0_kv), per_stream(conv0), per_stream(gdn0)]
                 + [_resident(a) for a in weights],
        out_specs=[pl.BlockSpec((bb, t, d), old)]
                  + [pl.BlockSpec((bb,) + s[1:], lambda i, c, n=len(s): (i,) + (0,) * (n - 1))
                     for s in state_shapes],
        out_shape=[jax.ShapeDtypeStruct((b, l, d), F32)]
                  + [jax.ShapeDtypeStruct(s, F32) for s in state_shapes],
        scratch_shapes=[pltpu.VMEM((bb, SUBLANES + t, R_PROJ), F32),
                        pltpu.VMEM((bb, SUBLANES + t, G_QKV), F32),
                        pltpu.VMEM((bb, R_PAIRS, R_PAIR * R_HEAD_DIM, R_PAIR * R_HEAD_DIM), F32),
                        pltpu.VMEM((bb, G_HEADS, G_HEAD_DIM, G_HEAD_DIM), F32),
                        pltpu.VMEM((bb, t, D_R), F32),
                        pltpu.VMEM((bb * t, D_R + D_G), BF16)],
        compiler_params=pltpu.CompilerParams(dimension_semantics=("arbitrary", "arbitrary"),
                                             vmem_limit_bytes=VMEM_LIMIT_BYTES),
        name="layer",
    )(x, x, p, shift0, wkv0_kv, conv0, gdn0, *weights)


def _pad_lanes(a, lanes, offset=0):
    out = jnp.zeros(a.shape[:-1] + (lanes,), a.dtype)
    return lax.dynamic_update_slice_in_dim(out, a, offset, axis=a.ndim - 1)


def _prepare_weights(wts, final_g):
    (ln_mix_g, w_in, mu_shift, w0, w_lora_up, a0, a_lora_up, g_lora_up, k_k, k_a, r_k, ln_x_w, ln_x_b,
     conv_w, a_log, dt_bias, gdn_norm_g, w_out, ln_ffn_g, w_gate, w_up, w_down, ln_ple_g,
     w_ple_gate, w_ple_proj) = wts
    row = lambda a: a.reshape(1, -1).astype(F32)
    off_z = R_PROJ + G_QKV
    off_b = off_z + D_G
    w_in_b = w_in.astype(BF16)
    zeros_lora = jnp.zeros((LORA_W, D_R), BF16)
    named = dict(
        lnm=row(ln_mix_g), wr=w_in_b[:, :R_PROJ], wq=w_in_b[:, R_PROJ:off_z], wz=w_in_b[:, off_z:off_b],
        wb=_pad_lanes(w_in_b[:, off_b:], LANES),
        mu=row(mu_shift), w0=row(w0),
        wlo=jnp.concatenate([w_lora_up.astype(BF16), zeros_lora], axis=0),
        alo=jnp.concatenate([zeros_lora, a_lora_up.astype(BF16)], axis=0),
        a0=row(a0), gup=g_lora_up.astype(BF16), kk=row(k_k), ka=row(k_a), rk=row(r_k),
        lnw=row(ln_x_w), lnb=row(ln_x_b),
        cw=conv_w, alog=_pad_lanes(row(a_log), LANES, G_HEADS), dtb=_pad_lanes(row(dt_bias), LANES, G_HEADS),
        ng=row(gdn_norm_g),
        wo=w_out.astype(BF16), lnf=row(ln_ffn_g), wgate=w_gate.astype(BF16), wup=w_up.astype(BF16),
        wdown=w_down.astype(BF16), lnp=row(ln_ple_g), wpg=w_ple_gate.astype(BF16),
        wpp=w_ple_proj.astype(BF16), fng=row(final_g))
    return tuple(named[n] for n in _WEIGHT_NAMES)


def _layer(x, p, shift0, wkv0, conv0, gdn0, weights):
    b = x.shape[0]
    y, shift1, wkv1_kv, conv1, gdn1 = _layer_call(x, p, shift0.reshape(b, 1, R_PROJ),
                                                  jnp.swapaxes(wkv0, -1, -2), conv0, gdn0, weights)
    return y, shift1.reshape(b, R_PROJ), jnp.swapaxes(wkv1_kv, -1, -2), conv1, gdn1


def kernel(x_prompt, x_sample, state_shift, state_wkv, state_conv, state_gdn, p_prompt, p_sample,
           ln_mix_g, w_in, mu_shift, w0, w_lora_up, a0, a_lora_up, g_lora_up, k_k, k_a, r_k,
           ln_x_w, ln_x_b, conv_w, a_log, dt_bias, gdn_norm_g, w_out, ln_ffn_g, w_gate, w_up,
           w_down, ln_ple_g, w_ple_gate, w_ple_proj, final_norm_g):
    depth = w_in.shape[0]
    assert depth == 1, "the final norm is fused into the single layer's kernel"
    bp = x_prompt.shape[0]
    dt = x_prompt.dtype
    wts = tuple(a[0] for a in (ln_mix_g, w_in, mu_shift, w0, w_lora_up, a0, a_lora_up, g_lora_up, k_k,
                               k_a, r_k, ln_x_w, ln_x_b, conv_w, a_log, dt_bias, gdn_norm_g, w_out,
                               ln_ffn_g, w_gate, w_up, w_down, ln_ple_g, w_ple_gate, w_ple_proj))
    weights = _prepare_weights(wts, final_norm_g)
    z_shift = jnp.zeros((bp, R_PROJ), dt)
    z_wkv = jnp.zeros((bp, R_HEADS, R_HEAD_DIM, R_HEAD_DIM), dt)
    z_conv = jnp.zeros((bp, CONV_W - 1, G_QKV), dt)
    z_gdn = jnp.zeros((bp, G_HEADS, G_HEAD_DIM, G_HEAD_DIM), dt)
    yp, a1, a2, a3, a4 = _layer(x_prompt, p_prompt[0], z_shift, z_wkv, z_conv, z_gdn, weights)
    ys, b1, b2, b3, b4 = _layer(x_sample, p_sample[0], state_shift[0], state_wkv[0], state_conv[0],
                                state_gdn[0], weights)
    stack = lambda a: a[None]
    return (yp, ys, stack(a1), stack(a2), stack(a3), stack(a4),
            stack(b1), stack(b2), stack(b3), stack(b4))
```

```python
import functools
import itertools
import math

import jax
import jax.numpy as jnp
from jax import lax
from jax.experimental import pallas as pl
from jax.experimental.pallas import tpu as pltpu

F32 = jnp.float32
BF16 = jnp.bfloat16

R_HEADS = 8
R_HEAD_DIM = 64
D_R = R_HEADS * R_HEAD_DIM
LORA_W = 64
LORA_A = 64
LORA_G = 128
R_PROJ = 3 * D_R + LORA_W + LORA_A + LORA_G
G_HEADS = 4
G_HEAD_DIM = 128
D_G = G_HEADS * G_HEAD_DIM
CONV_W = 4
G_QKV = 3 * D_G
CHUNK = 64
RMS_EPS = 1e-6
GN_EPS = 64e-5
DECAY_SCALE = math.exp(-0.5)

LANES = 128
SUBLANES = 8
VMEM_LIMIT_BYTES = 60 * 1024 * 1024
INV_BLOCK = 16
FF_TILE = 256
BATCH_BLOCK = 4
R_PAIR = LANES // R_HEAD_DIM
R_PAIRS = R_HEADS // R_PAIR


def _dot(a, b, nt=False):
    dn = (((1,), (1,)), ((), ())) if nt else (((1,), (0,)), ((), ()))
    return lax.dot_general(a, b, dn, preferred_element_type=F32)


def _mm1(a, b, nt=False):
    return _dot(a.astype(BF16), b.astype(BF16), nt)


def _mm_exact_lhs(a_bf16, b):
    b1 = b.astype(BF16)
    b2 = (b - b1.astype(F32)).astype(BF16)
    return _dot(a_bf16, b1) + _dot(a_bf16, b2)


def _sigmoid(x):
    return 0.5 * jnp.tanh(0.5 * x) + 0.5


def _softplus(x):
    return jnp.maximum(x, 0.0) + jnp.log1p(jnp.exp(-jnp.abs(x)))


def _rms(x, g):
    return x * lax.rsqrt(jnp.mean(x * x, axis=-1, keepdims=True) + RMS_EPS) * g


def _iota2(shape):
    return (lax.broadcasted_iota(jnp.int32, shape, 0), lax.broadcasted_iota(jnp.int32, shape, 1))


def _block_mm(width):
    n = width // INV_BLOCK
    row, col = _iota2((width, width))
    mask = (row // INV_BLOCK) == (col // INV_BLOCK)
    zero = jnp.zeros((), BF16)

    def mm(xs, ys):
        xb = [x.astype(BF16) for x in xs]
        yb = [y.astype(BF16) for y in ys]
        yb = [jnp.where(mask, jnp.concatenate([y] * n, axis=0), zero) for y in yb]
        return [_dot(x, y) for x, y in zip(xb, yb)]
    return mm


def _tri_inv(a_list, mm, tick):
    t, width = a_list[0].shape
    assert t % INV_BLOCK == 0 and t <= 4 * INV_BLOCK and width % t == 0
    nb = t // INV_BLOCK
    row_b, col_b = _iota2((INV_BLOCK, width))
    eye_b = (row_b == col_b % INV_BLOCK).astype(F32)
    lane_blk = (col_b % t) // INV_BLOCK
    d = []
    for a in a_list:
        acc = None
        for b in range(nb):
            piece = jnp.where(lane_blk == b, a[b * INV_BLOCK:(b + 1) * INV_BLOCK, :], 0.0)
            acc = piece if acc is None else acc + piece
        d.append(acc)
    mmb = _block_mm(width)
    d2 = mmb(d, d)
    tick()
    x = [eye_b - di for di in d]
    d4 = mmb(d2, d2)
    tick()
    x = [xi + pi for xi, pi in zip(x, mmb(x, d2))]
    tick()
    d8 = mmb(d4, d4)
    tick()
    x = [xi + pi for xi, pi in zip(x, mmb(x, d4))]
    tick()
    x = [xi + pi for xi, pi in zip(x, mmb(x, d8))]
    tick()
    if nb == 1:
        return x
    x = [jnp.concatenate([jnp.where(lane_blk == b, xi, 0.0) for b in range(nb)], axis=0) for xi in x]
    row, col = _iota2((t, width))
    col = col % t
    eye = (row == col).astype(F32)
    blk = (row // INV_BLOCK) == (col // INV_BLOCK)
    m = mm(x, [jnp.where(blk, 0.0, a) for a in a_list])
    tick()
    m2 = mm(m, m)
    tick()
    n = [eye - mi for mi in m]
    n = [ni + pi for ni, pi in zip(n, mm(n, m2))]
    tick()
    return mm(n, x)


def _pair_mm(t):
    row, col = _iota2((2 * t, 2 * t))
    mask = (row // t) == (col // t)
    zero = jnp.zeros((), BF16)

    def mm(xs, ys):
        xb = [x.astype(BF16) for x in xs]
        yb = [y.astype(BF16) for y in ys]
        yb = [jnp.where(mask, jnp.concatenate([y, y], axis=0), zero) for y in yb]
        return [_dot(x, y) for x, y in zip(xb, yb)]
    return mm


def _rwkv_mixer(t, bb, f_rows, w, fbuf, state, ybuf, rg, tick):
    base = SUBLANES
    hd = R_HEAD_DIM
    pw = R_PAIR * hd

    lane = lax.broadcasted_iota(jnp.int32, (t, pw), 1)
    first = lane < hd

    def segsum(*xs):
        outs = []
        for x in xs:
            blocks = []
            for p in range(R_PAIRS):
                xp = x[:, p * pw:(p + 1) * pw]
                s0 = jnp.sum(jnp.where(first, xp, 0.0), axis=-1, keepdims=True)
                s1 = jnp.sum(jnp.where(first, 0.0, xp), axis=-1, keepdims=True)
                blocks.append(jnp.where(first, s0, s1))
            outs.append(jnp.concatenate(blocks, axis=1))
        return outs

    row_t, col_t = _iota2((t, t))
    tri = (row_t >= col_t).astype(BF16)

    def prep(bi):
        f = f_rows[bi]
        fbuf[bi, base:base + t, :] = f
        fp = fbuf[bi, base - 1:base - 1 + t, :]
        fbuf[bi, base - 1:base, :] = f[t - 1:t, :]
        fm = f + (fp - f) * w.mu[...]

        r = fm[:, 0:D_R]
        k = fm[:, D_R:2 * D_R]
        v = fm[:, 2 * D_R:3 * D_R]
        wa = fm[:, 3 * D_R:3 * D_R + LORA_W + LORA_A]
        gl = fm[:, 3 * D_R + LORA_W + LORA_A:]

        ld = -DECAY_SCALE * _sigmoid(w.w0[...] + _mm1(jnp.tanh(wa), w.wlo[...]))
        a = _sigmoid(w.a0[...] + _mm1(wa, w.alo[...]))
        g = _mm1(_sigmoid(gl), w.gup[...])
        kk = k * w.kk[...]
        keff = k * (1.0 + (a - 1.0) * w.ka[...])
        kk_ss, bonus_dot = segsum(kk * kk, r * keff * w.rk[...])
        kk = kk * lax.rsqrt(kk_ss + 1e-12)
        kka = kk * a

        cum = _mm_exact_lhs(tri, ld)
        cl = cum[t - 1:t, :]
        e_mc = jnp.exp(-cum)
        e_lc = jnp.exp(cl - cum)
        stack_t = jnp.concatenate([keff * e_lc, kka * e_lc, jnp.broadcast_to(cl, (SUBLANES, D_R))],
                                  axis=0).T
        tick()
        return dict(
            rt=(r * jnp.exp(cum)).astype(BF16), kpt=(kk * jnp.exp(cum - ld)).astype(BF16),
            bt=(kka * e_mc).astype(BF16), ktl=(keff * e_mc).astype(BF16), vb=v.astype(BF16),
            kb_t=stack_t[:, :2 * t].astype(BF16),
            pt_col=jnp.exp(stack_t[:, 2 * t:2 * t + 1]),
            gate=g, bonus=bonus_dot * v)

    pre = [prep(bi) for bi in range(bb)]

    row_c, col_c = _iota2((R_PAIR * t, pw))
    m_ch = (row_c // t) == (col_c // hd)
    row_s, col_s = _iota2((pw, pw))
    m_s = (row_s // hd) == (col_s // hd)
    row_g, col_g = _iota2((t, 2 * R_PAIR * t))
    col_g = col_g % t
    strict_g = row_g > col_g
    incl_g = row_g >= col_g
    zero_b = jnp.zeros((), BF16)
    mm_pair = _pair_mm(t)

    def bd(x, mask):
        return jnp.where(mask, jnp.concatenate([x] * R_PAIR, axis=0), zero_b)

    units = [(bi, p) for bi in range(bb) for p in range(R_PAIRS)]
    n_u = range(len(units))
    pair = lambda name, u: pre[u[0]][name][:, u[1] * pw:(u[1] + 1) * pw]
    v_bd = [bd(pair("vb", u), m_ch) for u in units]
    lhs = [jnp.concatenate([pair("kpt", u), pair("rt", u)], axis=0) for u in units]
    rhs = [jnp.concatenate([bd(pair("bt", u), m_ch), bd(pair("ktl", u), m_ch)], axis=0)
           for u in units]
    gm = [_dot(lhs[i], rhs[i], nt=True) for i in n_u]
    tick()
    top = [jnp.where(strict_g, gm[i][:t], 0.0) for i in n_u]
    bot = [jnp.where(incl_g, gm[i][t:], 0.0).astype(BF16) for i in n_u]
    av = [_dot(top[i][:, R_PAIR * t:].astype(BF16), v_bd[i]) for i in n_u]
    tick()
    tinv = _tri_inv([top[i][:, :R_PAIR * t] for i in n_u], mm_pair, tick)
    tick()
    wu = [_dot(tinv[i].astype(BF16),
               jnp.concatenate([bd(pair("kpt", u), m_ch), bd(av[i].astype(BF16), m_ch)], axis=1))
          for i, u in enumerate(units)]
    tick()
    s = [state[u[0], u[1]] for u in units]
    ws = [_dot(jnp.concatenate([wu[i][:, :pw].astype(BF16), lhs[i][t:]], axis=0), s[i].astype(BF16))
          for i in n_u]
    tick()
    zb = [(ws[i][:t] + wu[i][:, pw:]).astype(BF16) for i in n_u]
    yz = [_dot(bot[i], jnp.concatenate([bd(-zb[i], m_ch), v_bd[i]], axis=0)) for i in n_u]
    upd = [_dot(pre[u[0]]["kb_t"][u[1] * pw:(u[1] + 1) * pw],
                jnp.concatenate([pair("vb", u), -zb[i]], axis=0)) for i, u in enumerate(units)]
    tick()
    for i, (bi, p) in enumerate(units):
        state[bi, p] = s[i] * pre[bi]["pt_col"][p * pw:(p + 1) * pw] + jnp.where(m_s, upd[i], 0.0)
        ybuf[bi, :, p * pw:(p + 1) * pw] = ws[i][t:] + yz[i]

    inv_n = 1.0 / hd
    for bi in range(bb):
        y = ybuf[bi]
        dlt = y - segsum(y)[0] * inv_n
        var = segsum(dlt * dlt)[0] * inv_n
        yn = dlt * lax.rsqrt(var + GN_EPS) * w.lnw[...] + w.lnb[...]
        rg[bi * t:(bi + 1) * t, 0:D_R] = ((yn + pre[bi]["bonus"]) * pre[bi]["gate"]).astype(rg.dtype)


def _gdn_mixer(t, bb, qkv_rows, z_rows, ba_rows, w, cbuf, state, rg, tick):
    base = SUBLANES
    hist = CONV_W - 1
    hdim = G_HEAD_DIM

    row, col = _iota2((t, t))
    tri = (row >= col).astype(BF16)
    mm = _mm1

    units = [(bi, h) for bi in range(bb) for h in range(G_HEADS)]
    qh, kh, vh, bcol, gcol, grow, grow2, glast, kt = {}, {}, {}, {}, {}, {}, {}, {}, {}
    for bi in range(bb):
        qkv = qkv_rows[bi]
        cbuf[bi, base:base + t, :] = qkv
        conv = qkv * w.cw[hist:hist + 1, :]
        for j in range(hist):
            conv = conv + cbuf[bi, base - hist + j:base - hist + j + t, :] * w.cw[j:j + 1, :]
        cbuf[bi, base - hist:base, :] = qkv[t - hist:t, :]
        conv = conv * _sigmoid(conv)

        ba = ba_rows[bi]
        beta_t = _sigmoid(ba)
        glog_t = -jnp.exp(w.alog[...]) * _softplus(ba + w.dtb[...])
        g_t = _mm_exact_lhs(tri, glog_t)
        g_tt = jnp.concatenate([g_t, g_t], axis=0).T
        k_norm = []
        for h in range(G_HEADS):
            sl = slice(h * hdim, (h + 1) * hdim)
            q = conv[:, sl]
            qh[bi, h] = q * lax.rsqrt(jnp.sum(q * q, axis=-1, keepdims=True) + 1e-6) * (hdim ** -0.5)
            k = conv[:, D_G + h * hdim:D_G + (h + 1) * hdim]
            kh[bi, h] = k * lax.rsqrt(jnp.sum(k * k, axis=-1, keepdims=True) + 1e-6)
            k_norm.append(kh[bi, h])
            vh[bi, h] = conv[:, 2 * D_G + h * hdim:2 * D_G + (h + 1) * hdim]
            bcol[bi, h] = beta_t[:, h:h + 1]
            gcol[bi, h] = g_t[:, G_HEADS + h:G_HEADS + h + 1]
            grow2[bi, h] = g_tt[G_HEADS + h:G_HEADS + h + 1, :]
            grow[bi, h] = grow2[bi, h][:, :t]
            glast[bi, h] = gcol[bi, h][t - 1:t, :]
        kt_all = jnp.concatenate(k_norm, axis=1).T
        for h in range(G_HEADS):
            kt[bi, h] = kt_all[h * hdim:(h + 1) * hdim]
        tick()

    pairs = [(bi, pp) for bi in range(bb) for pp in range(G_HEADS // 2)]
    n_p = range(len(pairs))
    heads_of = lambda pr: ((pr[0], 2 * pr[1]), (pr[0], 2 * pr[1] + 1))
    row_k, col_k = _iota2((2 * t, 2 * hdim))
    m_k = (row_k // t) == (col_k // hdim)
    top_rows = row_k < t
    row_p, col_p = _iota2((t, 2 * t))
    first = col_p < t
    col_p = col_p % t
    incl_p = row_p >= col_p
    strict_p = row_p > col_p
    zero_b = jnp.zeros((), BF16)
    mm_pair = _pair_mm(t)

    def bd_cols(x0, x1):
        xs = jnp.concatenate([x0, x1], axis=0)
        return jnp.concatenate([jnp.where(top_rows[:, :x0.shape[1]], xs, zero_b),
                                jnp.where(top_rows[:, :x0.shape[1]], zero_b, xs)], axis=1)

    kb, eg = {}, {}
    for u_ in units:
        kb[u_] = kh[u_] * bcol[u_]
        eg[u_] = jnp.exp(gcol[u_])
    dec_incl, lhs, rhs = [], [], []
    for pr in pairs:
        h0, h1 = heads_of(pr)
        dg = jnp.where(first, gcol[h0] - grow2[h0], gcol[h1] - grow2[h1])
        dec_incl.append(jnp.where(incl_p, jnp.exp(jnp.where(incl_p, dg, 0.0)), 0.0))
        lhs.append(jnp.concatenate([jnp.concatenate([kb[h0], kb[h1]], axis=1),
                                    jnp.concatenate([qh[h0], qh[h1]], axis=1)], axis=0).astype(BF16))
        khp = jnp.concatenate([kh[h0], kh[h1]], axis=1).astype(BF16)
        rhs.append(jnp.where(m_k, jnp.concatenate([khp, khp], axis=0), zero_b))
    gm = [_dot(lhs[i], rhs[i], nt=True) for i in n_p]
    a_mat = [gm[i][:t] * jnp.where(strict_p, dec_incl[i], 0.0) for i in n_p]
    qk = [(gm[i][t:] * dec_incl[i]).astype(BF16) for i in n_p]
    tick()
    tinv = _tri_inv(a_mat, mm_pair, tick)
    tick()
    uw_p = []
    for i, pr in enumerate(pairs):
        x0, x1 = [jnp.concatenate([vh[u_] * bcol[u_], kb[u_] * eg[u_]], axis=1).astype(BF16)
                  for u_ in heads_of(pr)]
        uw_p.append(_dot(tinv[i].astype(BF16), bd_cols(x0, x1)))
    tick()
    uw = [uw_p[i // 2][:, (i % 2) * 2 * hdim:(i % 2 + 1) * 2 * hdim] for i in range(len(units))]
    s = [state[u_[0], u_[1]] for u_ in units]
    wqs = [mm(jnp.concatenate([uw[i][:, hdim:], qh[u_] * eg[u_]], axis=0), s[i])
           for i, u_ in enumerate(units)]
    tick()
    v_new = [uw[i][:, :hdim] - wqs[i][:t] for i in range(len(units))]
    qkv_p = [_dot(qk[i], bd_cols(v_new[2 * i].astype(BF16), v_new[2 * i + 1].astype(BF16))) for i in n_p]
    qkv_new = [qkv_p[i // 2][:, (i % 2) * hdim:(i % 2 + 1) * hdim] for i in range(len(units))]
    kdv = [mm(kt[u_] * jnp.exp(glast[u_] - grow[u_]), v_new[i]) for i, u_ in enumerate(units)]
    tick()
    for i, (bi, h) in enumerate(units):
        sl = slice(h * hdim, (h + 1) * hdim)
        state[bi, h] = s[i] * jnp.exp(glast[bi, h]) + kdv[i]
        o = wqs[i][t:] + qkv_new[i]
        zh = z_rows[bi][:, sl]
        o = (o * lax.rsqrt(jnp.mean(o * o, axis=-1, keepdims=True) + RMS_EPS) * w.ng[...]
             * (zh * _sigmoid(zh)))
        rg[bi * t:(bi + 1) * t, D_R + h * hdim:D_R + (h + 1) * hdim] = o.astype(rg.dtype)


def _tail_steps(x, p, w, rg, y_ref, bb, t):
    x = x + _dot(rg[...], w.wo[...])
    yield
    hb = _rms(x, w.lnf[...]).astype(BF16)
    d_ff = w.wgate.shape[1]
    ffn = None
    for j in range(d_ff // FF_TILE):
        cs = slice(j * FF_TILE, (j + 1) * FF_TILE)
        gate = _dot(hb, w.wgate[:, cs])
        yield
        up = _dot(hb, w.wup[:, cs])
        yield
        act = (gate * _sigmoid(gate)) * up
        down = _dot(act.astype(BF16), w.wdown[cs, :])
        ffn = down if ffn is None else ffn + down
        yield
    x = x + ffn
    hb = _rms(x, w.lnp[...]).astype(BF16)
    ple_gate = _sigmoid(_dot(hb, w.wpg[...]))
    yield
    x = x + ple_gate * _dot(p.astype(BF16), w.wpp[...])
    y = _rms(x, w.fng[...])
    for bi in range(bb):
        y_ref[bi] = y[bi * t:(bi + 1) * t]
    yield


_WEIGHT_NAMES = ("lnm", "wr", "wq", "wz", "wb",
                 "mu", "w0", "wlo", "alo", "a0", "gup", "kk", "ka", "rk", "lnw", "lnb",
                 "cw", "alog", "dtb", "ng",
                 "wo", "lnf", "wgate", "wup", "wdown", "lnp", "wpg", "wpp", "fng")


class _Refs:
    def __init__(self, names, refs):
        for n, r in zip(names, refs):
            setattr(self, n, r)


def _layer_kernel(t, n_chunks, lag, x_ref, xl_ref, pl_ref, shift_ref, wkv0_ref, conv0_ref, gdn0_ref,
                  *rest):
    nw = len(_WEIGHT_NAMES)
    w = _Refs(_WEIGHT_NAMES, rest[:nw])
    y_ref, shift_out, wkv_out, conv_out, gdn_out = rest[nw:nw + 5]
    fbuf, cbuf, rstate, gstate, ybuf, rg = rest[nw + 5:]
    c_idx = pl.program_id(1)
    bb = x_ref.shape[0]
    base = SUBLANES
    hist = CONV_W - 1
    hd = R_HEAD_DIM

    @pl.when(c_idx == 0)
    def _():
        fbuf[:, base - 1:base, :] = shift_ref[...]
        cbuf[:, base - hist:base, :] = conv0_ref[...]
        gstate[...] = gdn0_ref[...]
        rstate[...] = jnp.zeros(rstate.shape, F32)
        for bi in range(bb):
            for h in range(R_HEADS):
                o = (h % R_PAIR) * hd
                rstate[bi, h // R_PAIR, o:o + hd, o:o + hd] = wkv0_ref[bi, h]
        rg[...] = jnp.zeros(rg.shape, rg.dtype)

    rows = lambda ref: jnp.concatenate([ref[bi] for bi in range(bb)], axis=0)
    tail = _tail_steps(rows(xl_ref), rows(pl_ref), w, rg, y_ref, bb, t)

    hb = _rms(rows(x_ref), w.lnm[...]).astype(BF16)
    split = lambda a: [a[bi * t:(bi + 1) * t] for bi in range(bb)]
    f_rows = split(_dot(hb, w.wr[...]))
    proj = {}

    def gdn_inproj():
        for name, ref in (("qkv", w.wq), ("z", w.wz), ("ba", w.wb)):
            proj[name] = split(_dot(hb, ref[...]))
            yield

    late = gdn_inproj()
    filler = itertools.chain(late, tail) if lag else late

    def tick():
        next(filler, None)

    if lag:
        next(tail)
    _rwkv_mixer(t, bb, f_rows, w, fbuf, rstate, ybuf, rg, tick)
    for _ in late:
        pass
    qkv_rows = proj["qkv"]
    _gdn_mixer(t, bb, qkv_rows, proj["z"], proj["ba"], w, cbuf, gstate, rg, tick)
    for _ in tail:
        pass

    @pl.when(c_idx == n_chunks - 1)
    def _():
        for bi in range(bb):
            shift_out[bi] = f_rows[bi][t - 1:t, :]
            conv_out[bi] = qkv_rows[bi][t - hist:t, :]
            for h in range(R_HEADS):
                o = (h % R_PAIR) * hd
                wkv_out[bi, h] = rstate[bi, h // R_PAIR, o:o + hd, o:o + hd]
        gdn_out[...] = gstate[...]


def _resident(a):
    nd = a.ndim
    return pl.BlockSpec(a.shape, lambda *_: (0,) * nd, pipeline_mode=pl.Buffered(1))


def _layer_call(x, p, shift0, wkv0_kv, conv0, gdn0, weights):
    b, l, d = x.shape
    t = min(CHUNK, l)
    nc = l // t
    lag = 1 if nc > 1 else 0
    bb = BATCH_BLOCK if b % BATCH_BLOCK == 0 else 1
    cur = lambda i, c: (i, jnp.minimum(c, nc - 1), 0)
    old = lambda i, c: (i, jnp.maximum(c - lag, 0), 0)
    per_stream = lambda a: pl.BlockSpec((bb,) + a.shape[1:], lambda i, c: (i,) + (0,) * (a.ndim - 1))
    state_shapes = (shift0.shape, wkv0_kv.shape, conv0.shape, gdn0.shape)
    return pl.pallas_call(
        functools.partial(_layer_kernel, t, nc, lag),
        grid=(b // bb, nc + lag),
        in_specs=[pl.BlockSpec((bb, t, d), cur), pl.BlockSpec((bb, t, d), old),
                  pl.BlockSpec((bb, t, p.shape[2]), old),
                  per_stream(shift0), per_stream(wkv0_kv), per_stream(conv0), per_stream(gdn0)]
                 + [_resident(a) for a in weights],
        out_specs=[pl.BlockSpec((bb, t, d), old)]
                  + [pl.BlockSpec((bb,) + s[1:], lambda i, c, n=len(s): (i,) + (0,) * (n - 1))
                     for s in state_shapes],
        out_shape=[jax.ShapeDtypeStruct((b, l, d), F32)]
                  + [jax.ShapeDtypeStruct(s, F32) for s in state_shapes],
        scratch_shapes=[pltpu.VMEM((bb, SUBLANES + t, R_PROJ), F32),
                        pltpu.VMEM((bb, SUBLANES + t, G_QKV), F32),
                        pltpu.VMEM((bb, R_PAIRS, R_PAIR * R_HEAD_DIM, R_PAIR * R_HEAD_DIM), F32),
                        pltpu.VMEM((bb, G_HEADS, G_HEAD_DIM, G_HEAD_DIM), F32),
                        pltpu.VMEM((bb, t, D_R), F32),
                        pltpu.VMEM((bb * t, D_R + D_G), BF16)],
        compiler_params=pltpu.CompilerParams(dimension_semantics=("arbitrary", "arbitrary"),
                                             vmem_limit_bytes=VMEM_LIMIT_BYTES),
        name="layer",
    )(x, x, p, shift0, wkv0_kv, conv0, gdn0, *weights)


def _pad_lanes(a, lanes, offset=0):
    out = jnp.zeros(a.shape[:-1] + (lanes,), a.dtype)
    return lax.dynamic_update_slice_in_dim(out, a, offset, axis=a.ndim - 1)


def _prepare_weights(wts, final_g):
    (ln_mix_g, w_in, mu_shift, w0, w_lora_up, a0, a_lora_up, g_lora_up, k_k, k_a, r_k, ln_x_w, ln_x_b,
     conv_w, a_log, dt_bias, gdn_norm_g, w_out, ln_ffn_g, w_gate, w_up, w_down, ln_ple_g,
     w_ple_gate, w_ple_proj) = wts
    row = lambda a: a.reshape(1, -1).astype(F32)
    off_z = R_PROJ + G_QKV
    off_b = off_z + D_G
    w_in_b = w_in.astype(BF16)
    zeros_lora = jnp.zeros((LORA_W, D_R), BF16)
    named = dict(
        lnm=row(ln_mix_g), wr=w_in_b[:, :R_PROJ], wq=w_in_b[:, R_PROJ:off_z], wz=w_in_b[:, off_z:off_b],
        wb=_pad_lanes(w_in_b[:, off_b:], LANES),
        mu=row(mu_shift), w0=row(w0),
        wlo=jnp.concatenate([w_lora_up.astype(BF16), zeros_lora], axis=0),
        alo=jnp.concatenate([zeros_lora, a_lora_up.astype(BF16)], axis=0),
        a0=row(a0), gup=g_lora_up.astype(BF16), kk=row(k_k), ka=row(k_a), rk=row(r_k),
        lnw=row(ln_x_w), lnb=row(ln_x_b),
        cw=conv_w, alog=_pad_lanes(row(a_log), LANES, G_HEADS), dtb=_pad_lanes(row(dt_bias), LANES, G_HEADS),
        ng=row(gdn_norm_g),
        wo=w_out.astype(BF16), lnf=row(ln_ffn_g), wgate=w_gate.astype(BF16), wup=w_up.astype(BF16),
        wdown=w_down.astype(BF16), lnp=row(ln_ple_g), wpg=w_ple_gate.astype(BF16),
        wpp=w_ple_proj.astype(BF16), fng=row(final_g))
    return tuple(named[n] for n in _WEIGHT_NAMES)


def _layer(x, p, shift0, wkv0, conv0, gdn0, weights):
    b = x.shape[0]
    y, shift1, wkv1_kv, conv1, gdn1 = _layer_call(x, p, shift0.reshape(b, 1, R_PROJ),
                                                  jnp.swapaxes(wkv0, -1, -2), conv0, gdn0, weights)
    return y, shift1.reshape(b, R_PROJ), jnp.swapaxes(wkv1_kv, -1, -2), conv1, gdn1


def kernel(x_prompt, x_sample, state_shift, state_wkv, state_conv, state_gdn, p_prompt, p_sample,
           ln_mix_g, w_in, mu_shift, w0, w_lora_up, a0, a_lora_up, g_lora_up, k_k, k_a, r_k,
           ln_x_w, ln_x_b, conv_w, a_log, dt_bias, gdn_norm_g, w_out, ln_ffn_g, w_gate, w_up,
           w_down, ln_ple_g, w_ple_gate, w_ple_proj, final_norm_g):
    depth = w_in.shape[0]
    assert depth == 1, "the final norm is fused into the single layer's kernel"
    bp = x_prompt.shape[0]
    dt = x_prompt.dtype
    wts = tuple(a[0] for a in (ln_mix_g, w_in, mu_shift, w0, w_lora_up, a0, a_lora_up, g_lora_up, k_k,
                               k_a, r_k, ln_x_w, ln_x_b, conv_w, a_log, dt_bias, gdn_norm_g, w_out,
                               ln_ffn_g, w_gate, w_up, w_down, ln_ple_g, w_ple_gate, w_ple_proj))
    weights = _prepare_weights(wts, final_norm_g)
    z_shift = jnp.zeros((bp, R_PROJ), dt)
    z_wkv = jnp.zeros((bp, R_HEADS, R_HEAD_DIM, R_HEAD_DIM), dt)
    z_conv = jnp.zeros((bp, CONV_W - 1, G_QKV), dt)
    z_gdn = jnp.zeros((bp, G_HEADS, G_HEAD_DIM, G_HEAD_DIM), dt)
    yp, a1, a2, a3, a4 = _layer(x_prompt, p_prompt[0], z_shift, z_wkv, z_conv, z_gdn, weights)
    ys, b1, b2, b3, b4 = _layer(x_sample, p_sample[0], state_shift[0], state_wkv[0], state_conv[0],
                                state_gdn[0], weights)
    stack = lambda a: a[None]
    return (yp, ys, stack(a1), stack(a2), stack(a3), stack(a4),
            stack(b1), stack(b2), stack(b3), stack(b4))
```

```python
import functools
import itertools
import math

import jax
import jax.numpy as jnp
from jax import lax
from jax.experimental import pallas as pl
from jax.experimental.pallas import tpu as pltpu

F32 = jnp.float32
BF16 = jnp.bfloat16

R_HEADS = 8
R_HEAD_DIM = 64
D_R = R_HEADS * R_HEAD_DIM
LORA_W = 64
LORA_A = 64
LORA_G = 128
R_PROJ = 3 * D_R + LORA_W + LORA_A + LORA_G
G_HEADS = 4
G_HEAD_DIM = 128
D_G = G_HEADS * G_HEAD_DIM
CONV_W = 4
G_QKV = 3 * D_G
CHUNK = 64
RMS_EPS = 1e-6
GN_EPS = 64e-5
DECAY_SCALE = math.exp(-0.5)

LANES = 128
SUBLANES = 8
VMEM_LIMIT_BYTES = 60 * 1024 * 1024
INV_BLOCK = 16
FF_TILE = 256
BATCH_BLOCK = 4
R_PAIR = LANES // R_HEAD_DIM
R_PAIRS = R_HEADS // R_PAIR


def _dot(a, b, nt=False):
    dn = (((1,), (1,)), ((), ())) if nt else (((1,), (0,)), ((), ()))
    return lax.dot_general(a, b, dn, preferred_element_type=F32)


def _mm1(a, b, nt=False):
    return _dot(a.astype(BF16), b.astype(BF16), nt)


def _mm_exact_lhs(a_bf16, b):
    b1 = b.astype(BF16)
    b2 = (b - b1.astype(F32)).astype(BF16)
    return _dot(a_bf16, b1) + _dot(a_bf16, b2)


def _sigmoid(x):
    return 0.5 * jnp.tanh(0.5 * x) + 0.5


def _softplus(x):
    return jnp.maximum(x, 0.0) + jnp.log1p(jnp.exp(-jnp.abs(x)))


def _rms(x, g):
    return x * lax.rsqrt(jnp.mean(x * x, axis=-1, keepdims=True) + RMS_EPS) * g


def _shift_rows(x, carry, k):
    rolled = pltpu.roll(x, k, 0)
    top_rows = lax.broadcasted_iota(jnp.int32, (SUBLANES, x.shape[1]), 0) < k
    top = jnp.where(top_rows, pltpu.roll(carry, k, 0), rolled[:SUBLANES])
    return jnp.concatenate([top, rolled[SUBLANES:]], axis=0)


def _iota2(shape):
    return (lax.broadcasted_iota(jnp.int32, shape, 0), lax.broadcasted_iota(jnp.int32, shape, 1))


def _block_mm(width):
    n = width // INV_BLOCK
    row, col = _iota2((width, width))
    mask = (row // INV_BLOCK) == (col // INV_BLOCK)
    zero = jnp.zeros((), BF16)

    def mm(xs, ys):
        xb = [x.astype(BF16) for x in xs]
        yb = [y.astype(BF16) for y in ys]
        yb = [jnp.where(mask, jnp.concatenate([y] * n, axis=0), zero) for y in yb]
        return [_dot(x, y) for x, y in zip(xb, yb)]
    return mm


def _tri_inv(a_list, mm, tick):
    t, width = a_list[0].shape
    assert t % INV_BLOCK == 0 and t <= 4 * INV_BLOCK and width % t == 0
    nb = t // INV_BLOCK
    row_b, col_b = _iota2((INV_BLOCK, width))
    eye_b = (row_b == col_b % INV_BLOCK).astype(F32)
    lane_blk = (col_b % t) // INV_BLOCK
    d = []
    for a in a_list:
        acc = None
        for b in range(nb):
            piece = jnp.where(lane_blk == b, a[b * INV_BLOCK:(b + 1) * INV_BLOCK, :], 0.0)
            acc = piece if acc is None else acc + piece
        d.append(acc)
    mmb = _block_mm(width)
    d2 = mmb(d, d)
    tick()
    x = [eye_b - di for di in d]
    d4 = mmb(d2, d2)
    tick()
    x = [xi + pi for xi, pi in zip(x, mmb(x, d2))]
    tick()
    d8 = mmb(d4, d4)
    tick()
    x = [xi + pi for xi, pi in zip(x, mmb(x, d4))]
    tick()
    x = [xi + pi for xi, pi in zip(x, mmb(x, d8))]
    tick()
    if nb == 1:
        return x
    x = [jnp.concatenate([jnp.where(lane_blk == b, xi, 0.0) for b in range(nb)], axis=0) for xi in x]
    row, col = _iota2((t, width))
    col = col % t
    eye = (row == col).astype(F32)
    blk = (row // INV_BLOCK) == (col // INV_BLOCK)
    m = mm(x, [jnp.where(blk, 0.0, a) for a in a_list])
    tick()
    m2 = mm(m, m)
    tick()
    n = [eye - mi for mi in m]
    n = [ni + pi for ni, pi in zip(n, mm(n, m2))]
    tick()
    return mm(n, x)


def _pair_mm(t):
    row, col = _iota2((2 * t, 2 * t))
    mask = (row // t) == (col // t)
    zero = jnp.zeros((), BF16)

    def mm(xs, ys):
        xb = [x.astype(BF16) for x in xs]
        yb = [y.astype(BF16) for y in ys]
        yb = [jnp.where(mask, jnp.concatenate([y, y], axis=0), zero) for y in yb]
        return [_dot(x, y) for x, y in zip(xb, yb)]
    return mm


def _rwkv_mixer(t, bb, f_rows, w, fbuf, state, ybuf, rg, tick):
    base = SUBLANES
    hd = R_HEAD_DIM
    pw = R_PAIR * hd

    lane = lax.broadcasted_iota(jnp.int32, (t, pw), 1)
    first = lane < hd

    def segsum(*xs):
        outs = []
        for x in xs:
            blocks = []
            for p in range(R_PAIRS):
                xp = x[:, p * pw:(p + 1) * pw]
                s0 = jnp.sum(jnp.where(first, xp, 0.0), axis=-1, keepdims=True)
                s1 = jnp.sum(jnp.where(first, 0.0, xp), axis=-1, keepdims=True)
                blocks.append(jnp.where(first, s0, s1))
            outs.append(jnp.concatenate(blocks, axis=1))
        return outs

    row_t, col_t = _iota2((t, t))
    tri = (row_t >= col_t).astype(BF16)

    def prep(bi):
        f = f_rows[bi]
        fp = _shift_rows(f, fbuf[bi], 1)
        fbuf[bi, base - 1:base, :] = f[t - 1:t, :]
        fm = f + (fp - f) * w.mu[...]

        r = fm[:, 0:D_R]
        k = fm[:, D_R:2 * D_R]
        v = fm[:, 2 * D_R:3 * D_R]
        wa = fm[:, 3 * D_R:3 * D_R + LORA_W + LORA_A]
        gl = fm[:, 3 * D_R + LORA_W + LORA_A:]

        ld = -DECAY_SCALE * _sigmoid(w.w0[...] + _mm1(jnp.tanh(wa), w.wlo[...]))
        a = _sigmoid(w.a0[...] + _mm1(wa, w.alo[...]))
        g = _mm1(_sigmoid(gl), w.gup[...])
        kk = k * w.kk[...]
        keff = k * (1.0 + (a - 1.0) * w.ka[...])
        kk_ss, bonus_dot = segsum(kk * kk, r * keff * w.rk[...])
        kk = kk * lax.rsqrt(kk_ss + 1e-12)
        kka = kk * a

        cum = _mm_exact_lhs(tri, ld)
        cl = cum[t - 1:t, :]
        e_mc = jnp.exp(-cum)
        e_lc = jnp.exp(cl - cum)
        stack_t = jnp.concatenate([keff * e_lc, kka * e_lc, jnp.broadcast_to(cl, (SUBLANES, D_R))],
                                  axis=0).T
        tick()
        return dict(
            rt=(r * jnp.exp(cum)).astype(BF16), kpt=(kk * jnp.exp(cum - ld)).astype(BF16),
            bt=(kka * e_mc).astype(BF16), ktl=(keff * e_mc).astype(BF16), vb=v.astype(BF16),
            kb_t=stack_t[:, :2 * t].astype(BF16),
            pt_col=jnp.exp(stack_t[:, 2 * t:2 * t + 1]),
            gate=g, bonus=bonus_dot * v)

    pre = [prep(bi) for bi in range(bb)]

    row_c, col_c = _iota2((R_PAIR * t, pw))
    m_ch = (row_c // t) == (col_c // hd)
    row_s, col_s = _iota2((pw, pw))
    m_s = (row_s // hd) == (col_s // hd)
    row_g, col_g = _iota2((t, 2 * R_PAIR * t))
    col_g = col_g % t
    strict_g = row_g > col_g
    incl_g = row_g >= col_g
    zero_b = jnp.zeros((), BF16)
    mm_pair = _pair_mm(t)

    def bd(x, mask):
        return jnp.where(mask, jnp.concatenate([x] * R_PAIR, axis=0), zero_b)

    units = [(bi, p) for bi in range(bb) for p in range(R_PAIRS)]
    n_u = range(len(units))
    pair = lambda name, u: pre[u[0]][name][:, u[1] * pw:(u[1] + 1) * pw]
    v_bd = [bd(pair("vb", u), m_ch) for u in units]
    lhs = [jnp.concatenate([pair("kpt", u), pair("rt", u)], axis=0) for u in units]
    rhs = [jnp.concatenate([bd(pair("bt", u), m_ch), bd(pair("ktl", u), m_ch)], axis=0)
           for u in units]
    gm = [_dot(lhs[i], rhs[i], nt=True) for i in n_u]
    tick()
    top = [jnp.where(strict_g, gm[i][:t], 0.0) for i in n_u]
    bot = [jnp.where(incl_g, gm[i][t:], 0.0).astype(BF16) for i in n_u]
    av = [_dot(top[i][:, R_PAIR * t:].astype(BF16), v_bd[i]) for i in n_u]
    tick()
    tinv = _tri_inv([top[i][:, :R_PAIR * t] for i in n_u], mm_pair, tick)
    tick()
    wu = [_dot(tinv[i].astype(BF16),
               jnp.concatenate([bd(pair("kpt", u), m_ch), bd(av[i].astype(BF16), m_ch)], axis=1))
          for i, u in enumerate(units)]
    tick()
    s = [state[u[0], u[1]] for u in units]
    ws = [_dot(jnp.concatenate([wu[i][:, :pw].astype(BF16), lhs[i][t:]], axis=0), s[i].astype(BF16))
          for i in n_u]
    tick()
    zb = [(ws[i][:t] + wu[i][:, pw:]).astype(BF16) for i in n_u]
    yz = [_dot(bot[i], jnp.concatenate([bd(-zb[i], m_ch), v_bd[i]], axis=0)) for i in n_u]
    upd = [_dot(pre[u[0]]["kb_t"][u[1] * pw:(u[1] + 1) * pw],
                jnp.concatenate([pair("vb", u), -zb[i]], axis=0)) for i, u in enumerate(units)]
    tick()
    for i, (bi, p) in enumerate(units):
        state[bi, p] = s[i] * pre[bi]["pt_col"][p * pw:(p + 1) * pw] + jnp.where(m_s, upd[i], 0.0)
        ybuf[bi, :, p * pw:(p + 1) * pw] = ws[i][t:] + yz[i]

    inv_n = 1.0 / hd
    for bi in range(bb):
        y = ybuf[bi]
        dlt = y - segsum(y)[0] * inv_n
        var = segsum(dlt * dlt)[0] * inv_n
        yn = dlt * lax.rsqrt(var + GN_EPS) * w.lnw[...] + w.lnb[...]
        rg[bi * t:(bi + 1) * t, 0:D_R] = ((yn + pre[bi]["bonus"]) * pre[bi]["gate"]).astype(rg.dtype)


def _gdn_mixer(t, bb, qkv_rows, z_rows, ba_rows, w, cbuf, state, rg, tick):
    base = SUBLANES
    hist = CONV_W - 1
    hdim = G_HEAD_DIM

    row, col = _iota2((t, t))
    tri = (row >= col).astype(BF16)
    mm = _mm1

    units = [(bi, h) for bi in range(bb) for h in range(G_HEADS)]
    qh, kh, vh, bcol, gcol, grow, grow2, glast, kt = {}, {}, {}, {}, {}, {}, {}, {}, {}
    for bi in range(bb):
        qkv = qkv_rows[bi]
        carry = cbuf[bi]
        conv = qkv * w.cw[hist:hist + 1, :]
        for j in range(hist):
            conv = conv + _shift_rows(qkv, carry, hist - j) * w.cw[j:j + 1, :]
        cbuf[bi, base - hist:base, :] = qkv[t - hist:t, :]
        conv = conv * _sigmoid(conv)

        ba = ba_rows[bi]
        beta_t = _sigmoid(ba)
        glog_t = -jnp.exp(w.alog[...]) * _softplus(ba + w.dtb[...])
        g_t = _mm_exact_lhs(tri, glog_t)
        g_tt = jnp.concatenate([g_t, g_t], axis=0).T
        k_norm = []
        for h in range(G_HEADS):
            sl = slice(h * hdim, (h + 1) * hdim)
            q = conv[:, sl]
            qh[bi, h] = q * lax.rsqrt(jnp.sum(q * q, axis=-1, keepdims=True) + 1e-6) * (hdim ** -0.5)
            k = conv[:, D_G + h * hdim:D_G + (h + 1) * hdim]
            kh[bi, h] = k * lax.rsqrt(jnp.sum(k * k, axis=-1, keepdims=True) + 1e-6)
            k_norm.append(kh[bi, h])
            vh[bi, h] = conv[:, 2 * D_G + h * hdim:2 * D_G + (h + 1) * hdim]
            bcol[bi, h] = beta_t[:, h:h + 1]
            gcol[bi, h] = g_t[:, G_HEADS + h:G_HEADS + h + 1]
            grow2[bi, h] = g_tt[G_HEADS + h:G_HEADS + h + 1, :]
            grow[bi, h] = grow2[bi, h][:, :t]
            glast[bi, h] = gcol[bi, h][t - 1:t, :]
        kt_all = jnp.concatenate(k_norm, axis=1).T
        for h in range(G_HEADS):
            kt[bi, h] = kt_all[h * hdim:(h + 1) * hdim]
        tick()

    pairs = [(bi, pp) for bi in range(bb) for pp in range(G_HEADS // 2)]
    n_p = range(len(pairs))
    heads_of = lambda pr: ((pr[0], 2 * pr[1]), (pr[0], 2 * pr[1] + 1))
    row_k, col_k = _iota2((2 * t, 2 * hdim))
    m_k = (row_k // t) == (col_k // hdim)
    top_rows = row_k < t
    row_p, col_p = _iota2((t, 2 * t))
    first = col_p < t
    col_p = col_p % t
    incl_p = row_p >= col_p
    strict_p = row_p > col_p
    zero_b = jnp.zeros((), BF16)
    mm_pair = _pair_mm(t)

    def bd_cols(x0, x1):
        xs = jnp.concatenate([x0, x1], axis=0)
        return jnp.concatenate([jnp.where(top_rows[:, :x0.shape[1]], xs, zero_b),
                                jnp.where(top_rows[:, :x0.shape[1]], zero_b, xs)], axis=1)

    kb, eg = {}, {}
    for u_ in units:
        kb[u_] = kh[u_] * bcol[u_]
        eg[u_] = jnp.exp(gcol[u_])
    dec_incl, lhs, rhs = [], [], []
    for pr in pairs:
        h0, h1 = heads_of(pr)
        dg = jnp.where(first, gcol[h0] - grow2[h0], gcol[h1] - grow2[h1])
        dec_incl.append(jnp.where(incl_p, jnp.exp(jnp.where(incl_p, dg, 0.0)), 0.0))
        lhs.append(jnp.concatenate([jnp.concatenate([kb[h0], kb[h1]], axis=1),
                                    jnp.concatenate([qh[h0], qh[h1]], axis=1)], axis=0).astype(BF16))
        khp = jnp.concatenate([kh[h0], kh[h1]], axis=1).astype(BF16)
        rhs.append(jnp.where(m_k, jnp.concatenate([khp, khp], axis=0), zero_b))
    gm = [_dot(lhs[i], rhs[i], nt=True) for i in n_p]
    a_mat = [gm[i][:t] * jnp.where(strict_p, dec_incl[i], 0.0) for i in n_p]
    qk = [(gm[i][t:] * dec_incl[i]).astype(BF16) for i in n_p]
    tick()
    tinv = _tri_inv(a_mat, mm_pair, tick)
    tick()
    uw_p = []
    for i, pr in enumerate(pairs):
        x0, x1 = [jnp.concatenate([vh[u_] * bcol[u_], kb[u_] * eg[u_]], axis=1).astype(BF16)
                  for u_ in heads_of(pr)]
        uw_p.append(_dot(tinv[i].astype(BF16), bd_cols(x0, x1)))
    tick()
    uw = [uw_p[i // 2][:, (i % 2) * 2 * hdim:(i % 2 + 1) * 2 * hdim] for i in range(len(units))]
    s = [state[u_[0], u_[1]] for u_ in units]
    wqs = [mm(jnp.concatenate([uw[i][:, hdim:], qh[u_] * eg[u_]], axis=0), s[i])
           for i, u_ in enumerate(units)]
    tick()
    v_new = [uw[i][:, :hdim] - wqs[i][:t] for i in range(len(units))]
    qkv_p = [_dot(qk[i], bd_cols(v_new[2 * i].astype(BF16), v_new[2 * i + 1].astype(BF16))) for i in n_p]
    qkv_new = [qkv_p[i // 2][:, (i % 2) * hdim:(i % 2 + 1) * hdim] for i in range(len(units))]
    kdv = [mm(kt[u_] * jnp.exp(glast[u_] - grow[u_]), v_new[i]) for i, u_ in enumerate(units)]
    tick()
    for i, (bi, h) in enumerate(units):
        sl = slice(h * hdim, (h + 1) * hdim)
        state[bi, h] = s[i] * jnp.exp(glast[bi, h]) + kdv[i]
        o = wqs[i][t:] + qkv_new[i]
        zh = z_rows[bi][:, sl]
        o = (o * lax.rsqrt(jnp.mean(o * o, axis=-1, keepdims=True) + RMS_EPS) * w.ng[...]
             * (zh * _sigmoid(zh)))
        rg[bi * t:(bi + 1) * t, D_R + h * hdim:D_R + (h + 1) * hdim] = o.astype(rg.dtype)


def _tail_steps(x, p, w, rg, y_ref, bb, t):
    x = x + _dot(rg[...], w.wo[...])
    yield
    hb = _rms(x, w.lnf[...]).astype(BF16)
    d_ff = w.wgate.shape[1]
    ffn = None
    for j in range(d_ff // FF_TILE):
        cs = slice(j * FF_TILE, (j + 1) * FF_TILE)
        gate = _dot(hb, w.wgate[:, cs])
        yield
        up = _dot(hb, w.wup[:, cs])
        yield
        act = (gate * _sigmoid(gate)) * up
        down = _dot(act.astype(BF16), w.wdown[cs, :])
        ffn = down if ffn is None else ffn + down
        yield
    x = x + ffn
    hb = _rms(x, w.lnp[...]).astype(BF16)
    ple_gate = _sigmoid(_dot(hb, w.wpg[...]))
    yield
    x = x + ple_gate * _dot(p.astype(BF16), w.wpp[...])
    y = _rms(x, w.fng[...])
    for bi in range(bb):
        y_ref[bi] = y[bi * t:(bi + 1) * t]
    yield


_WEIGHT_NAMES = ("lnm", "win",
                 "mu", "w0", "wlo", "alo", "a0", "gup", "kk", "ka", "rk", "lnw", "lnb",
                 "cw", "alog", "dtb", "ng",
                 "wo", "lnf", "wgate", "wup", "wdown", "lnp", "wpg", "wpp", "fng")


class _Refs:
    def __init__(self, names, refs):
        for n, r in zip(names, refs):
            setattr(self, n, r)


def _layer_kernel(t, n_chunks, lag, x_ref, xl_ref, pl_ref, shift_ref, wkv0_ref, conv0_ref, gdn0_ref,
                  *rest):
    nw = len(_WEIGHT_NAMES)
    w = _Refs(_WEIGHT_NAMES, rest[:nw])
    y_ref, shift_out, wkv_out, conv_out, gdn_out = rest[nw:nw + 5]
    fbuf, cbuf, rstate, gstate, ybuf, rg = rest[nw + 5:]
    c_idx = pl.program_id(1)
    bb = x_ref.shape[0]
    base = SUBLANES
    hist = CONV_W - 1
    hd = R_HEAD_DIM

    @pl.when(c_idx == 0)
    def _():
        fbuf[...] = jnp.zeros(fbuf.shape, F32)
        cbuf[...] = jnp.zeros(cbuf.shape, F32)
        fbuf[:, base - 1:base, :] = shift_ref[...]
        cbuf[:, base - hist:base, :] = conv0_ref[...]
        gstate[...] = gdn0_ref[...]
        rstate[...] = jnp.zeros(rstate.shape, F32)
        for bi in range(bb):
            for h in range(R_HEADS):
                o = (h % R_PAIR) * hd
                rstate[bi, h // R_PAIR, o:o + hd, o:o + hd] = wkv0_ref[bi, h]
        rg[...] = jnp.zeros(rg.shape, rg.dtype)

    rows = lambda ref: jnp.concatenate([ref[bi] for bi in range(bb)], axis=0)
    tail = _tail_steps(rows(xl_ref), rows(pl_ref), w, rg, y_ref, bb, t)

    hb = _rms(rows(x_ref), w.lnm[...]).astype(BF16)
    split = lambda a: [a[bi * t:(bi + 1) * t] for bi in range(bb)]
    f_rows = split(_dot(hb, w.win[:, 0:R_PROJ]))
    proj = {}

    def gdn_inproj():
        off_z = R_PROJ + G_QKV
        off_b = off_z + D_G
        for name, c0, c1 in (("qkv", R_PROJ, off_z), ("z", off_z, off_b), ("ba", off_b, off_b + LANES)):
            proj[name] = split(_dot(hb, w.win[:, c0:c1]))
            yield

    late = gdn_inproj()
    filler = itertools.chain(late, tail) if lag else late

    def tick():
        next(filler, None)

    if lag:
        next(tail)
    _rwkv_mixer(t, bb, f_rows, w, fbuf, rstate, ybuf, rg, tick)
    for _ in late:
        pass
    qkv_rows = proj["qkv"]
    _gdn_mixer(t, bb, qkv_rows, proj["z"], proj["ba"], w, cbuf, gstate, rg, tick)
    for _ in tail:
        pass

    @pl.when(c_idx == n_chunks - 1)
    def _():
        for bi in range(bb):
            shift_out[bi] = f_rows[bi][t - 1:t, :]
            conv_out[bi] = qkv_rows[bi][t - hist:t, :]
            for h in range(R_HEADS):
                o = (h % R_PAIR) * hd
                wkv_out[bi, h] = rstate[bi, h // R_PAIR, o:o + hd, o:o + hd]
        gdn_out[...] = gstate[...]


def _resident(a):
    nd = a.ndim
    return pl.BlockSpec(a.shape, lambda *_: (0,) * nd, pipeline_mode=pl.Buffered(1))


def _layer_call(x, p, shift0, wkv0_kv, conv0, gdn0, weights):
    b, l, d = x.shape
    t = min(CHUNK, l)
    nc = l // t
    lag = 1 if nc > 1 else 0
    bb = BATCH_BLOCK if b % BATCH_BLOCK == 0 else 1
    cur = lambda i, c: (i, jnp.minimum(c, nc - 1), 0)
    old = lambda i, c: (i, jnp.maximum(c - lag, 0), 0)
    per_stream = lambda a: pl.BlockSpec((bb,) + a.shape[1:], lambda i, c: (i,) + (0,) * (a.ndim - 1))
    state_shapes = (shift0.shape, wkv0_kv.shape, conv0.shape, gdn0.shape)
    return pl.pallas_call(
        functools.partial(_layer_kernel, t, nc, lag),
        grid=(b // bb, nc + lag),
        in_specs=[pl.BlockSpec((bb, t, d), cur), pl.BlockSpec((bb, t, d), old),
                  pl.BlockSpec((bb, t, p.shape[2]), old),
                  per_stream(shift0), per_stream(wkv0_kv), per_stream(conv0), per_stream(gdn0)]
                 + [_resident(a) for a in weights],
        out_specs=[pl.BlockSpec((bb, t, d), old)]
                  + [pl.BlockSpec((bb,) + s[1:], lambda i, c, n=len(s): (i,) + (0,) * (n - 1))
                     for s in state_shapes],
        out_shape=[jax.ShapeDtypeStruct((b, l, d), F32)]
                  + [jax.ShapeDtypeStruct(s, F32) for s in state_shapes],
        scratch_shapes=[pltpu.VMEM((bb, SUBLANES, R_PROJ), F32),
                        pltpu.VMEM((bb, SUBLANES, G_QKV), F32),
                        pltpu.VMEM((bb, R_PAIRS, R_PAIR * R_HEAD_DIM, R_PAIR * R_HEAD_DIM), F32),
                        pltpu.VMEM((bb, G_HEADS, G_HEAD_DIM, G_HEAD_DIM), F32),
                        pltpu.VMEM((bb, t, D_R), F32),
                        pltpu.VMEM((bb * t, D_R + D_G), BF16)],
        compiler_params=pltpu.CompilerParams(dimension_semantics=("arbitrary", "arbitrary"),
                                             vmem_limit_bytes=VMEM_LIMIT_BYTES),
        name="layer",
    )(x, x, p, shift0, wkv0_kv, conv0, gdn0, *weights)


def _pad_lanes(a, lanes, offset=0):
    out = jnp.zeros(a.shape[:-1] + (lanes,), a.dtype)
    return lax.dynamic_update_slice_in_dim(out, a, offset, axis=a.ndim - 1)


def _prepare_weights(wts, final_g):
    (ln_mix_g, w_in, mu_shift, w0, w_lora_up, a0, a_lora_up, g_lora_up, k_k, k_a, r_k, ln_x_w, ln_x_b,
     conv_w, a_log, dt_bias, gdn_norm_g, w_out, ln_ffn_g, w_gate, w_up, w_down, ln_ple_g,
     w_ple_gate, w_ple_proj) = wts
    row = lambda a: a.reshape(1, -1).astype(F32)
    d_in_padded = R_PROJ + G_QKV + D_G + LANES
    zeros_lora = jnp.zeros((LORA_W, D_R), BF16)
    named = dict(
        lnm=row(ln_mix_g), win=_pad_lanes(w_in.astype(BF16), d_in_padded),
        mu=row(mu_shift), w0=row(w0),
        wlo=jnp.concatenate([w_lora_up.astype(BF16), zeros_lora], axis=0),
        alo=jnp.concatenate([zeros_lora, a_lora_up.astype(BF16)], axis=0),
        a0=row(a0), gup=g_lora_up.astype(BF16), kk=row(k_k), ka=row(k_a), rk=row(r_k),
        lnw=row(ln_x_w), lnb=row(ln_x_b),
        cw=conv_w, alog=_pad_lanes(row(a_log), LANES, G_HEADS), dtb=_pad_lanes(row(dt_bias), LANES, G_HEADS),
        ng=row(gdn_norm_g),
        wo=w_out.astype(BF16), lnf=row(ln_ffn_g), wgate=w_gate.astype(BF16), wup=w_up.astype(BF16),
        wdown=w_down.astype(BF16), lnp=row(ln_ple_g), wpg=w_ple_gate.astype(BF16),
        wpp=w_ple_proj.astype(BF16), fng=row(final_g))
    return tuple(named[n] for n in _WEIGHT_NAMES)


def _layer(x, p, shift0, wkv0, conv0, gdn0, weights):
    b = x.shape[0]
    y, shift1, wkv1_kv, conv1, gdn1 = _layer_call(x, p, shift0.reshape(b, 1, R_PROJ),
                                                  jnp.swapaxes(wkv0, -1, -2), conv0, gdn0, weights)
    return y, shift1.reshape(b, R_PROJ), jnp.swapaxes(wkv1_kv, -1, -2), conv1, gdn1


def kernel(x_prompt, x_sample, state_shift, state_wkv, state_conv, state_gdn, p_prompt, p_sample,
           ln_mix_g, w_in, mu_shift, w0, w_lora_up, a0, a_lora_up, g_lora_up, k_k, k_a, r_k,
           ln_x_w, ln_x_b, conv_w, a_log, dt_bias, gdn_norm_g, w_out, ln_ffn_g, w_gate, w_up,
           w_down, ln_ple_g, w_ple_gate, w_ple_proj, final_norm_g):
    depth = w_in.shape[0]
    assert depth == 1, "the final norm is fused into the single layer's kernel"
    bp = x_prompt.shape[0]
    dt = x_prompt.dtype
    wts = tuple(a[0] for a in (ln_mix_g, w_in, mu_shift, w0, w_lora_up, a0, a_lora_up, g_lora_up, k_k,
                               k_a, r_k, ln_x_w, ln_x_b, conv_w, a_log, dt_bias, gdn_norm_g, w_out,
                               ln_ffn_g, w_gate, w_up, w_down, ln_ple_g, w_ple_gate, w_ple_proj))
    weights = _prepare_weights(wts, final_norm_g)
    z_shift = jnp.zeros((bp, R_PROJ), dt)
    z_wkv = jnp.zeros((bp, R_HEADS, R_HEAD_DIM, R_HEAD_DIM), dt)
    z_conv = jnp.zeros((bp, CONV_W - 1, G_QKV), dt)
    z_gdn = jnp.zeros((bp, G_HEADS, G_HEAD_DIM, G_HEAD_DIM), dt)
    yp, a1, a2, a3, a4 = _layer(x_prompt, p_prompt[0], z_shift, z_wkv, z_conv, z_gdn, weights)
    ys, b1, b2, b3, b4 = _layer(x_sample, p_sample[0], state_shift[0], state_wkv[0], state_conv[0],
                                state_gdn[0], weights)
    stack = lambda a: a[None]
    return (yp, ys, stack(a1), stack(a2), stack(a3), stack(a4),
            stack(b1), stack(b2), stack(b3), stack(b4))
```

```python
import functools
import math

import jax
import jax.numpy as jnp
from jax import lax
from jax.experimental import pallas as pl
from jax.experimental.pallas import tpu as pltpu

F32 = jnp.float32
BF16 = jnp.bfloat16

R_HEADS = 8
R_HEAD_DIM = 64
D_R = R_HEADS * R_HEAD_DIM
LORA_W = 64
LORA_A = 64
LORA_G = 128
R_PROJ = 3 * D_R + LORA_W + LORA_A + LORA_G
G_HEADS = 4
G_HEAD_DIM = 128
D_G = G_HEADS * G_HEAD_DIM
CONV_W = 4
G_QKV = 3 * D_G
CHUNK = 64
RMS_EPS = 1e-6
GN_EPS = 64e-5
DECAY_SCALE = math.exp(-0.5)

LANES = 128
SUBLANES = 8
VMEM_LIMIT_BYTES = 60 * 1024 * 1024
INV_BLOCK = 16
FF_TILE = 256
BATCH_BLOCK = 4
TAIL_TILE = 1024
R_PAIR = LANES // R_HEAD_DIM
R_PAIRS = R_HEADS // R_PAIR


def _dot(a, b, nt=False):
    dn = (((1,), (1,)), ((), ())) if nt else (((1,), (0,)), ((), ()))
    return lax.dot_general(a, b, dn, preferred_element_type=F32)


def _mm1(a, b, nt=False):
    return _dot(a.astype(BF16), b.astype(BF16), nt)


def _mm_exact_lhs(a_bf16, b):
    b1 = b.astype(BF16)
    b2 = (b - b1.astype(F32)).astype(BF16)
    return _dot(a_bf16, b1) + _dot(a_bf16, b2)


def _sigmoid(x):
    return 0.5 * jnp.tanh(0.5 * x) + 0.5


def _softplus(x):
    return jnp.maximum(x, 0.0) + jnp.log1p(jnp.exp(-jnp.abs(x)))


def _rms(x, g):
    return x * lax.rsqrt(jnp.mean(x * x, axis=-1, keepdims=True) + RMS_EPS) * g


def _shift_rows(x, carry, k):
    rolled = pltpu.roll(x, k, 0)
    top_rows = lax.broadcasted_iota(jnp.int32, (SUBLANES, x.shape[1]), 0) < k
    top = jnp.where(top_rows, pltpu.roll(carry, k, 0), rolled[:SUBLANES])
    return jnp.concatenate([top, rolled[SUBLANES:]], axis=0)


def _iota2(shape):
    return (lax.broadcasted_iota(jnp.int32, shape, 0), lax.broadcasted_iota(jnp.int32, shape, 1))


def _block_mm(width):
    n = width // INV_BLOCK
    row, col = _iota2((width, width))
    mask = (row // INV_BLOCK) == (col // INV_BLOCK)
    zero = jnp.zeros((), BF16)

    def mm(xs, ys):
        xb = [x.astype(BF16) for x in xs]
        yb = [y.astype(BF16) for y in ys]
        yb = [jnp.where(mask, jnp.concatenate([y] * n, axis=0), zero) for y in yb]
        return [_dot(x, y) for x, y in zip(xb, yb)]
    return mm


def _tri_inv(a_list, mm, tick):
    t, width = a_list[0].shape
    assert t % INV_BLOCK == 0 and t <= 4 * INV_BLOCK and width % t == 0
    nb = t // INV_BLOCK
    row_b, col_b = _iota2((INV_BLOCK, width))
    eye_b = (row_b == col_b % INV_BLOCK).astype(F32)
    lane_blk = (col_b % t) // INV_BLOCK
    d = []
    for a in a_list:
        acc = None
        for b in range(nb):
            piece = jnp.where(lane_blk == b, a[b * INV_BLOCK:(b + 1) * INV_BLOCK, :], 0.0)
            acc = piece if acc is None else acc + piece
        d.append(acc)
    mmb = _block_mm(width)
    d2 = mmb(d, d)
    tick()
    x = [eye_b - di for di in d]
    d4 = mmb(d2, d2)
    tick()
    x = [xi + pi for xi, pi in zip(x, mmb(x, d2))]
    tick()
    d8 = mmb(d4, d4)
    tick()
    x = [xi + pi for xi, pi in zip(x, mmb(x, d4))]
    tick()
    x = [xi + pi for xi, pi in zip(x, mmb(x, d8))]
    tick()
    if nb == 1:
        return x
    x = [jnp.concatenate([jnp.where(lane_blk == b, xi, 0.0) for b in range(nb)], axis=0) for xi in x]
    row, col = _iota2((t, width))
    col = col % t
    eye = (row == col).astype(F32)
    blk = (row // INV_BLOCK) == (col // INV_BLOCK)
    m = mm(x, [jnp.where(blk, 0.0, a) for a in a_list])
    tick()
    m2 = mm(m, m)
    tick()
    n = [eye - mi for mi in m]
    n = [ni + pi for ni, pi in zip(n, mm(n, m2))]
    tick()
    return mm(n, x)


def _pair_mm(t):
    row, col = _iota2((2 * t, 2 * t))
    mask = (row // t) == (col // t)
    zero = jnp.zeros((), BF16)

    def mm(xs, ys):
        xb = [x.astype(BF16) for x in xs]
        yb = [y.astype(BF16) for y in ys]
        yb = [jnp.where(mask, jnp.concatenate([y, y], axis=0), zero) for y in yb]
        return [_dot(x, y) for x, y in zip(xb, yb)]
    return mm


def _rwkv_mixer(t, bb, f_rows, w, fbuf, state, ybuf, rg, tick):
    base = SUBLANES
    hd = R_HEAD_DIM
    pw = R_PAIR * hd

    lane = lax.broadcasted_iota(jnp.int32, (t, pw), 1)
    first = lane < hd

    def segsum(*xs):
        outs = []
        for x in xs:
            blocks = []
            for p in range(R_PAIRS):
                xp = x[:, p * pw:(p + 1) * pw]
                s0 = jnp.sum(jnp.where(first, xp, 0.0), axis=-1, keepdims=True)
                s1 = jnp.sum(jnp.where(first, 0.0, xp), axis=-1, keepdims=True)
                blocks.append(jnp.where(first, s0, s1))
            outs.append(jnp.concatenate(blocks, axis=1))
        return outs

    row_t, col_t = _iota2((t, t))
    tri = (row_t >= col_t).astype(BF16)

    def prep(bi):
        f = f_rows[bi]
        fp = _shift_rows(f, fbuf[bi], 1)
        fbuf[bi, base - 1:base, :] = f[t - 1:t, :]
        fm = f + (fp - f) * w.mu[...]

        r = fm[:, 0:D_R]
        k = fm[:, D_R:2 * D_R]
        v = fm[:, 2 * D_R:3 * D_R]
        wa = fm[:, 3 * D_R:3 * D_R + LORA_W + LORA_A]
        gl = fm[:, 3 * D_R + LORA_W + LORA_A:]

        ld = -DECAY_SCALE * _sigmoid(w.w0[...] + _mm1(jnp.tanh(wa), w.wlo[...]))
        a = _sigmoid(w.a0[...] + _mm1(wa, w.alo[...]))
        g = _mm1(_sigmoid(gl), w.gup[...])
        kk = k * w.kk[...]
        keff = k * (1.0 + (a - 1.0) * w.ka[...])
        kk_ss, bonus_dot = segsum(kk * kk, r * keff * w.rk[...])
        kk = kk * lax.rsqrt(kk_ss + 1e-12)
        kka = kk * a

        cum = _mm_exact_lhs(tri, ld)
        cl = cum[t - 1:t, :]
        e_mc = jnp.exp(-cum)
        e_lc = jnp.exp(cl - cum)
        stack_t = jnp.concatenate([keff * e_lc, kka * e_lc, jnp.broadcast_to(cl, (SUBLANES, D_R))],
                                  axis=0).T
        tick()
        return dict(
            rt=(r * jnp.exp(cum)).astype(BF16), kpt=(kk * jnp.exp(cum - ld)).astype(BF16),
            bt=(kka * e_mc).astype(BF16), ktl=(keff * e_mc).astype(BF16), vb=v.astype(BF16),
            kb_t=stack_t[:, :2 * t].astype(BF16),
            pt_col=jnp.exp(stack_t[:, 2 * t:2 * t + 1]),
            gate=g, bonus=bonus_dot * v)

    pre = [prep(bi) for bi in range(bb)]

    row_c, col_c = _iota2((R_PAIR * t, pw))
    m_ch = (row_c // t) == (col_c // hd)
    row_s, col_s = _iota2((pw, pw))
    m_s = (row_s // hd) == (col_s // hd)
    row_g, col_g = _iota2((t, 2 * R_PAIR * t))
    col_g = col_g % t
    strict_g = row_g > col_g
    incl_g = row_g >= col_g
    zero_b = jnp.zeros((), BF16)
    mm_pair = _pair_mm(t)

    def bd(x, mask):
        return jnp.where(mask, jnp.concatenate([x] * R_PAIR, axis=0), zero_b)

    units = [(bi, p) for bi in range(bb) for p in range(R_PAIRS)]
    n_u = range(len(units))
    pair = lambda name, u: pre[u[0]][name][:, u[1] * pw:(u[1] + 1) * pw]
    v_bd = [bd(pair("vb", u), m_ch) for u in units]
    lhs = [jnp.concatenate([pair("kpt", u), pair("rt", u)], axis=0) for u in units]
    rhs = [jnp.concatenate([bd(pair("bt", u), m_ch), bd(pair("ktl", u), m_ch)], axis=0)
           for u in units]
    gm = [_dot(lhs[i], rhs[i], nt=True) for i in n_u]
    tick()
    top = [jnp.where(strict_g, gm[i][:t], 0.0) for i in n_u]
    bot = [jnp.where(incl_g, gm[i][t:], 0.0).astype(BF16) for i in n_u]
    av = [_dot(top[i][:, R_PAIR * t:].astype(BF16), v_bd[i]) for i in n_u]
    tick()
    tinv = _tri_inv([top[i][:, :R_PAIR * t] for i in n_u], mm_pair, tick)
    tick()
    wu = [_dot(tinv[i].astype(BF16),
               jnp.concatenate([bd(pair("kpt", u), m_ch), bd(av[i].astype(BF16), m_ch)], axis=1))
          for i, u in enumerate(units)]
    tick()
    s = [state[u[0], u[1]] for u in units]
    ws = [_dot(jnp.concatenate([wu[i][:, :pw].astype(BF16), lhs[i][t:]], axis=0), s[i].astype(BF16))
          for i in n_u]
    tick()
    zb = [(ws[i][:t] + wu[i][:, pw:]).astype(BF16) for i in n_u]
    yz = [_dot(bot[i], jnp.concatenate([bd(-zb[i], m_ch), v_bd[i]], axis=0)) for i in n_u]
    upd = [_dot(pre[u[0]]["kb_t"][u[1] * pw:(u[1] + 1) * pw],
                jnp.concatenate([pair("vb", u), -zb[i]], axis=0)) for i, u in enumerate(units)]
    tick()
    for i, (bi, p) in enumerate(units):
        state[bi, p] = s[i] * pre[bi]["pt_col"][p * pw:(p + 1) * pw] + jnp.where(m_s, upd[i], 0.0)
        ybuf[bi, :, p * pw:(p + 1) * pw] = ws[i][t:] + yz[i]

    inv_n = 1.0 / hd
    for bi in range(bb):
        y = ybuf[bi]
        dlt = y - segsum(y)[0] * inv_n
        var = segsum(dlt * dlt)[0] * inv_n
        yn = dlt * lax.rsqrt(var + GN_EPS) * w.lnw[...] + w.lnb[...]
        rg[bi * t:(bi + 1) * t, 0:D_R] = ((yn + pre[bi]["bonus"]) * pre[bi]["gate"]).astype(rg.dtype)


def _gdn_mixer(t, bb, qkv_rows, z_rows, ba_rows, w, cbuf, state, rg, tick):
    base = SUBLANES
    hist = CONV_W - 1
    hdim = G_HEAD_DIM

    row, col = _iota2((t, t))
    tri = (row >= col).astype(BF16)
    mm = _mm1

    units = [(bi, h) for bi in range(bb) for h in range(G_HEADS)]
    qh, kh, vh, bcol, gcol, grow, grow2, glast, kt = {}, {}, {}, {}, {}, {}, {}, {}, {}
    for bi in range(bb):
        qkv = qkv_rows[bi]
        carry = cbuf[bi]
        conv = qkv * w.cw[hist:hist + 1, :]
        for j in range(hist):
            conv = conv + _shift_rows(qkv, carry, hist - j) * w.cw[j:j + 1, :]
        cbuf[bi, base - hist:base, :] = qkv[t - hist:t, :]
        conv = conv * _sigmoid(conv)

        ba = ba_rows[bi]
        beta_t = _sigmoid(ba)
        glog_t = -jnp.exp(w.alog[...]) * _softplus(ba + w.dtb[...])
        g_t = _mm_exact_lhs(tri, glog_t)
        g_tt = jnp.concatenate([g_t, g_t], axis=0).T
        k_norm = []
        for h in range(G_HEADS):
            sl = slice(h * hdim, (h + 1) * hdim)
            q = conv[:, sl]
            qh[bi, h] = q * lax.rsqrt(jnp.sum(q * q, axis=-1, keepdims=True) + 1e-6) * (hdim ** -0.5)
            k = conv[:, D_G + h * hdim:D_G + (h + 1) * hdim]
            kh[bi, h] = k * lax.rsqrt(jnp.sum(k * k, axis=-1, keepdims=True) + 1e-6)
            k_norm.append(kh[bi, h])
            vh[bi, h] = conv[:, 2 * D_G + h * hdim:2 * D_G + (h + 1) * hdim]
            bcol[bi, h] = beta_t[:, h:h + 1]
            gcol[bi, h] = g_t[:, G_HEADS + h:G_HEADS + h + 1]
            grow2[bi, h] = g_tt[G_HEADS + h:G_HEADS + h + 1, :]
            grow[bi, h] = grow2[bi, h][:, :t]
            glast[bi, h] = gcol[bi, h][t - 1:t, :]
        kt_all = jnp.concatenate(k_norm, axis=1).T
        for h in range(G_HEADS):
            kt[bi, h] = kt_all[h * hdim:(h + 1) * hdim]
        tick()

    pairs = [(bi, pp) for bi in range(bb) for pp in range(G_HEADS // 2)]
    n_p = range(len(pairs))
    heads_of = lambda pr: ((pr[0], 2 * pr[1]), (pr[0], 2 * pr[1] + 1))
    row_k, col_k = _iota2((2 * t, 2 * hdim))
    m_k = (row_k // t) == (col_k // hdim)
    top_rows = row_k < t
    row_p, col_p = _iota2((t, 2 * t))
    first = col_p < t
    col_p = col_p % t
    incl_p = row_p >= col_p
    strict_p = row_p > col_p
    zero_b = jnp.zeros((), BF16)
    mm_pair = _pair_mm(t)

    def bd_cols(x0, x1):
        xs = jnp.concatenate([x0, x1], axis=0)
        return jnp.concatenate([jnp.where(top_rows[:, :x0.shape[1]], xs, zero_b),
                                jnp.where(top_rows[:, :x0.shape[1]], zero_b, xs)], axis=1)

    kb, eg = {}, {}
    for u_ in units:
        kb[u_] = kh[u_] * bcol[u_]
        eg[u_] = jnp.exp(gcol[u_])
    dec_incl, lhs, rhs = [], [], []
    for pr in pairs:
        h0, h1 = heads_of(pr)
        dg = jnp.where(first, gcol[h0] - grow2[h0], gcol[h1] - grow2[h1])
        dec_incl.append(jnp.where(incl_p, jnp.exp(jnp.where(incl_p, dg, 0.0)), 0.0))
        lhs.append(jnp.concatenate([jnp.concatenate([kb[h0], kb[h1]], axis=1),
                                    jnp.concatenate([qh[h0], qh[h1]], axis=1)], axis=0).astype(BF16))
        khp = jnp.concatenate([kh[h0], kh[h1]], axis=1).astype(BF16)
        rhs.append(jnp.where(m_k, jnp.concatenate([khp, khp], axis=0), zero_b))
    gm = [_dot(lhs[i], rhs[i], nt=True) for i in n_p]
    a_mat = [gm[i][:t] * jnp.where(strict_p, dec_incl[i], 0.0) for i in n_p]
    qk = [(gm[i][t:] * dec_incl[i]).astype(BF16) for i in n_p]
    tick()
    tinv = _tri_inv(a_mat, mm_pair, tick)
    tick()
    uw_p = []
    for i, pr in enumerate(pairs):
        x0, x1 = [jnp.concatenate([vh[u_] * bcol[u_], kb[u_] * eg[u_]], axis=1).astype(BF16)
                  for u_ in heads_of(pr)]
        uw_p.append(_dot(tinv[i].astype(BF16), bd_cols(x0, x1)))
    tick()
    uw = [uw_p[i // 2][:, (i % 2) * 2 * hdim:(i % 2 + 1) * 2 * hdim] for i in range(len(units))]
    s = [state[u_[0], u_[1]] for u_ in units]
    wqs = [mm(jnp.concatenate([uw[i][:, hdim:], qh[u_] * eg[u_]], axis=0), s[i])
           for i, u_ in enumerate(units)]
    tick()
    v_new = [uw[i][:, :hdim] - wqs[i][:t] for i in range(len(units))]
    qkv_p = [_dot(qk[i], bd_cols(v_new[2 * i].astype(BF16), v_new[2 * i + 1].astype(BF16))) for i in n_p]
    qkv_new = [qkv_p[i // 2][:, (i % 2) * hdim:(i % 2 + 1) * hdim] for i in range(len(units))]
    kdv = [mm(kt[u_] * jnp.exp(glast[u_] - grow[u_]), v_new[i]) for i, u_ in enumerate(units)]
    tick()
    for i, (bi, h) in enumerate(units):
        sl = slice(h * hdim, (h + 1) * hdim)
        state[bi, h] = s[i] * jnp.exp(glast[bi, h]) + kdv[i]
        o = wqs[i][t:] + qkv_new[i]
        zh = z_rows[bi][:, sl]
        o = (o * lax.rsqrt(jnp.mean(o * o, axis=-1, keepdims=True) + RMS_EPS) * w.ng[...]
             * (zh * _sigmoid(zh)))
        rg[bi * t:(bi + 1) * t, D_R + h * hdim:D_R + (h + 1) * hdim] = o.astype(rg.dtype)


_TAIL_WEIGHTS = ("wo", "lnf", "wgate", "wup", "wdown", "lnp", "wpg", "wpp", "fng")


def _tail_kernel(x_ref, rg_ref, p_ref, *rest):
    w = _Refs(_TAIL_WEIGHTS, rest[:len(_TAIL_WEIGHTS)])
    y_ref = rest[len(_TAIL_WEIGHTS)]
    x = x_ref[...] + _dot(rg_ref[...], w.wo[...])
    hb = _rms(x, w.lnf[...]).astype(BF16)
    d_ff = w.wgate.shape[1]
    ffn = None
    for j in range(d_ff // FF_TILE):
        cs = slice(j * FF_TILE, (j + 1) * FF_TILE)
        gate = _dot(hb, w.wgate[:, cs])
        up = _dot(hb, w.wup[:, cs])
        act = (gate * _sigmoid(gate)) * up
        down = _dot(act.astype(BF16), w.wdown[cs, :])
        ffn = down if ffn is None else ffn + down
    x = x + ffn
    hb = _rms(x, w.lnp[...]).astype(BF16)
    x = x + _sigmoid(_dot(hb, w.wpg[...])) * _dot(p_ref[...].astype(BF16), w.wpp[...])
    y_ref[...] = _rms(x, w.fng[...])


def _tail_call(x2, rg2, p2, weights):
    n, d = x2.shape
    tm = min(TAIL_TILE, n)
    tok = lambda width: pl.BlockSpec((tm, width), lambda i: (i, 0))
    return pl.pallas_call(
        _tail_kernel,
        grid=(n // tm,),
        in_specs=[tok(d), tok(rg2.shape[1]), tok(p2.shape[1])] + [_resident(a) for a in weights],
        out_specs=tok(d),
        out_shape=jax.ShapeDtypeStruct((n, d), F32),
        compiler_params=pltpu.CompilerParams(dimension_semantics=("arbitrary",),
                                             vmem_limit_bytes=VMEM_LIMIT_BYTES),
        name="tail",
    )(x2, rg2, p2, *weights)


_WEIGHT_NAMES = ("lnm", "win",
                 "mu", "w0", "wlo", "alo", "a0", "gup", "kk", "ka", "rk", "lnw", "lnb",
                 "cw", "alog", "dtb", "ng")


class _Refs:
    def __init__(self, names, refs):
        for n, r in zip(names, refs):
            setattr(self, n, r)


def _layer_kernel(t, n_chunks, x_ref, shift_ref, wkv0_ref, conv0_ref, gdn0_ref, *rest):
    nw = len(_WEIGHT_NAMES)
    w = _Refs(_WEIGHT_NAMES, rest[:nw])
    rg_out, shift_out, wkv_out, conv_out, gdn_out = rest[nw:nw + 5]
    fbuf, cbuf, rstate, gstate, ybuf, rg = rest[nw + 5:]
    c_idx = pl.program_id(1)
    bb = x_ref.shape[0]
    base = SUBLANES
    hist = CONV_W - 1
    hd = R_HEAD_DIM

    @pl.when(c_idx == 0)
    def _():
        fbuf[...] = jnp.zeros(fbuf.shape, F32)
        cbuf[...] = jnp.zeros(cbuf.shape, F32)
        fbuf[:, base - 1:base, :] = shift_ref[...]
        cbuf[:, base - hist:base, :] = conv0_ref[...]
        gstate[...] = gdn0_ref[...]
        rstate[...] = jnp.zeros(rstate.shape, F32)
        for bi in range(bb):
            for h in range(R_HEADS):
                o = (h % R_PAIR) * hd
                rstate[bi, h // R_PAIR, o:o + hd, o:o + hd] = wkv0_ref[bi, h]

    rows = lambda ref: jnp.concatenate([ref[bi] for bi in range(bb)], axis=0)

    hb = _rms(rows(x_ref), w.lnm[...]).astype(BF16)
    split = lambda a: [a[bi * t:(bi + 1) * t] for bi in range(bb)]
    f_rows = split(_dot(hb, w.win[:, 0:R_PROJ]))
    proj = {}

    def gdn_inproj():
        off_z = R_PROJ + G_QKV
        off_b = off_z + D_G
        for name, c0, c1 in (("qkv", R_PROJ, off_z), ("z", off_z, off_b), ("ba", off_b, off_b + LANES)):
            proj[name] = split(_dot(hb, w.win[:, c0:c1]))
            yield

    late = gdn_inproj()

    def tick():
        next(late, None)

    _rwkv_mixer(t, bb, f_rows, w, fbuf, rstate, ybuf, rg, tick)
    for _ in late:
        pass
    qkv_rows = proj["qkv"]
    _gdn_mixer(t, bb, qkv_rows, proj["z"], proj["ba"], w, cbuf, gstate, rg, tick)
    for bi in range(bb):
        rg_out[bi] = rg[bi * t:(bi + 1) * t, :]

    @pl.when(c_idx == n_chunks - 1)
    def _():
        for bi in range(bb):
            shift_out[bi] = f_rows[bi][t - 1:t, :]
            conv_out[bi] = qkv_rows[bi][t - hist:t, :]
            for h in range(R_HEADS):
                o = (h % R_PAIR) * hd
                wkv_out[bi, h] = rstate[bi, h // R_PAIR, o:o + hd, o:o + hd]
        gdn_out[...] = gstate[...]


def _resident(a):
    nd = a.ndim
    return pl.BlockSpec(a.shape, lambda *_: (0,) * nd, pipeline_mode=pl.Buffered(1))


def _layer_call(x, shift0, wkv0_kv, conv0, gdn0, weights):
    b, l, d = x.shape
    t = min(CHUNK, l)
    nc = l // t
    bb = BATCH_BLOCK if b % BATCH_BLOCK == 0 else 1
    cur = lambda i, c: (i, c, 0)
    per_stream = lambda a: pl.BlockSpec((bb,) + a.shape[1:], lambda i, c: (i,) + (0,) * (a.ndim - 1))
    state_shapes = (shift0.shape, wkv0_kv.shape, conv0.shape, gdn0.shape)
    return pl.pallas_call(
        functools.partial(_layer_kernel, t, nc),
        grid=(b // bb, nc),
        in_specs=[pl.BlockSpec((bb, t, d), cur),
                  per_stream(shift0), per_stream(wkv0_kv), per_stream(conv0), per_stream(gdn0)]
                 + [_resident(a) for a in weights],
        out_specs=[pl.BlockSpec((bb, t, D_R + D_G), cur)]
                  + [pl.BlockSpec((bb,) + s[1:], lambda i, c, n=len(s): (i,) + (0,) * (n - 1))
                     for s in state_shapes],
        out_shape=[jax.ShapeDtypeStruct((b, l, D_R + D_G), BF16)]
                  + [jax.ShapeDtypeStruct(s, F32) for s in state_shapes],
        scratch_shapes=[pltpu.VMEM((bb, SUBLANES, R_PROJ), F32),
                        pltpu.VMEM((bb, SUBLANES, G_QKV), F32),
                        pltpu.VMEM((bb, R_PAIRS, R_PAIR * R_HEAD_DIM, R_PAIR * R_HEAD_DIM), F32),
                        pltpu.VMEM((bb, G_HEADS, G_HEAD_DIM, G_HEAD_DIM), F32),
                        pltpu.VMEM((bb, t, D_R), F32),
                        pltpu.VMEM((bb * t, D_R + D_G), BF16)],
        compiler_params=pltpu.CompilerParams(dimension_semantics=("arbitrary", "arbitrary"),
                                             vmem_limit_bytes=VMEM_LIMIT_BYTES),
        name="mixers",
    )(x, shift0, wkv0_kv, conv0, gdn0, *weights)


def _pad_lanes(a, lanes, offset=0):
    out = jnp.zeros(a.shape[:-1] + (lanes,), a.dtype)
    return lax.dynamic_update_slice_in_dim(out, a, offset, axis=a.ndim - 1)


def _prepare_weights(wts, final_g):
    (ln_mix_g, w_in, mu_shift, w0, w_lora_up, a0, a_lora_up, g_lora_up, k_k, k_a, r_k, ln_x_w, ln_x_b,
     conv_w, a_log, dt_bias, gdn_norm_g, w_out, ln_ffn_g, w_gate, w_up, w_down, ln_ple_g,
     w_ple_gate, w_ple_proj) = wts
    row = lambda a: a.reshape(1, -1).astype(F32)
    d_in_padded = R_PROJ + G_QKV + D_G + LANES
    zeros_lora = jnp.zeros((LORA_W, D_R), BF16)
    named = dict(
        lnm=row(ln_mix_g), win=_pad_lanes(w_in.astype(BF16), d_in_padded),
        mu=row(mu_shift), w0=row(w0),
        wlo=jnp.concatenate([w_lora_up.astype(BF16), zeros_lora], axis=0),
        alo=jnp.concatenate([zeros_lora, a_lora_up.astype(BF16)], axis=0),
        a0=row(a0), gup=g_lora_up.astype(BF16), kk=row(k_k), ka=row(k_a), rk=row(r_k),
        lnw=row(ln_x_w), lnb=row(ln_x_b),
        cw=conv_w, alog=_pad_lanes(row(a_log), LANES, G_HEADS), dtb=_pad_lanes(row(dt_bias), LANES, G_HEADS),
        ng=row(gdn_norm_g),
        wo=w_out.astype(BF16), lnf=row(ln_ffn_g), wgate=w_gate.astype(BF16), wup=w_up.astype(BF16),
        wdown=w_down.astype(BF16), lnp=row(ln_ple_g), wpg=w_ple_gate.astype(BF16),
        wpp=w_ple_proj.astype(BF16), fng=row(final_g))
    return (tuple(named[n] for n in _WEIGHT_NAMES), tuple(named[n] for n in _TAIL_WEIGHTS))


def _layer(x, p, shift0, wkv0, conv0, gdn0, weights):
    b, l, d = x.shape
    mixer_w, tail_w = weights
    rg, shift1, wkv1_kv, conv1, gdn1 = _layer_call(x, shift0.reshape(b, 1, R_PROJ),
                                                   jnp.swapaxes(wkv0, -1, -2), conv0, gdn0, mixer_w)
    y = _tail_call(x.reshape(b * l, d), rg.reshape(b * l, D_R + D_G), p.reshape(b * l, -1), tail_w)
    return y.reshape(b, l, d), shift1.reshape(b, R_PROJ), jnp.swapaxes(wkv1_kv, -1, -2), conv1, gdn1


def kernel(x_prompt, x_sample, state_shift, state_wkv, state_conv, state_gdn, p_prompt, p_sample,
           ln_mix_g, w_in, mu_shift, w0, w_lora_up, a0, a_lora_up, g_lora_up, k_k, k_a, r_k,
           ln_x_w, ln_x_b, conv_w, a_log, dt_bias, gdn_norm_g, w_out, ln_ffn_g, w_gate, w_up,
           w_down, ln_ple_g, w_ple_gate, w_ple_proj, final_norm_g):
    depth = w_in.shape[0]
    assert depth == 1, "the final norm is fused into the single layer's kernel"
    bp = x_prompt.shape[0]
    dt = x_prompt.dtype
    wts = tuple(a[0] for a in (ln_mix_g, w_in, mu_shift, w0, w_lora_up, a0, a_lora_up, g_lora_up, k_k,
                               k_a, r_k, ln_x_w, ln_x_b, conv_w, a_log, dt_bias, gdn_norm_g, w_out,
                               ln_ffn_g, w_gate, w_up, w_down, ln_ple_g, w_ple_gate, w_ple_proj))
    weights = _prepare_weights(wts, final_norm_g)
    z_shift = jnp.zeros((bp, R_PROJ), dt)
    z_wkv = jnp.zeros((bp, R_HEADS, R_HEAD_DIM, R_HEAD_DIM), dt)
    z_conv = jnp.zeros((bp, CONV_W - 1, G_QKV), dt)
    z_gdn = jnp.zeros((bp, G_HEADS, G_HEAD_DIM, G_HEAD_DIM), dt)
    yp, a1, a2, a3, a4 = _layer(x_prompt, p_prompt[0], z_shift, z_wkv, z_conv, z_gdn, weights)
    ys, b1, b2, b3, b4 = _layer(x_sample, p_sample[0], state_shift[0], state_wkv[0], state_conv[0],
                                state_gdn[0], weights)
    stack = lambda a: a[None]
    return (yp, ys, stack(a1), stack(a2), stack(a3), stack(a4),
            stack(b1), stack(b2), stack(b3), stack(b4))
```

```python
import functools
import math

import jax
import jax.numpy as jnp
from jax import lax
from jax.experimental import pallas as pl
from jax.experimental.pallas import tpu as pltpu

F32 = jnp.float32
BF16 = jnp.bfloat16

R_HEADS = 8
R_HEAD_DIM = 64
D_R = R_HEADS * R_HEAD_DIM
LORA_W = 64
LORA_A = 64
LORA_G = 128
R_PROJ = 3 * D_R + LORA_W + LORA_A + LORA_G
G_HEADS = 4
G_HEAD_DIM = 128
D_G = G_HEADS * G_HEAD_DIM
CONV_W = 4
G_QKV = 3 * D_G
CHUNK = 64
RMS_EPS = 1e-6
GN_EPS = 64e-5
DECAY_SCALE = math.exp(-0.5)

LANES = 128
SUBLANES = 8
VMEM_LIMIT_BYTES = 60 * 1024 * 1024
INV_BLOCK = 16
FF_TILE = 256
BATCH_BLOCK = 4
TAIL_TILE = 1024
R_PAIR = LANES // R_HEAD_DIM
R_PAIRS = R_HEADS // R_PAIR


def _dot(a, b, nt=False):
    dn = (((1,), (1,)), ((), ())) if nt else (((1,), (0,)), ((), ()))
    return lax.dot_general(a, b, dn, preferred_element_type=F32)


def _mm1(a, b, nt=False):
    return _dot(a.astype(BF16), b.astype(BF16), nt)


def _mm_exact_lhs(a_bf16, b):
    b1 = b.astype(BF16)
    b2 = (b - b1.astype(F32)).astype(BF16)
    return _dot(a_bf16, b1) + _dot(a_bf16, b2)


def _sigmoid(x):
    return 0.5 * jnp.tanh(0.5 * x) + 0.5


def _softplus(x):
    return jnp.maximum(x, 0.0) + jnp.log1p(jnp.exp(-jnp.abs(x)))


def _rms(x, g):
    return x * lax.rsqrt(jnp.mean(x * x, axis=-1, keepdims=True) + RMS_EPS) * g


def _shift_rows(x, carry, k):
    rolled = pltpu.roll(x, k, 0)
    top_rows = lax.broadcasted_iota(jnp.int32, (SUBLANES, x.shape[1]), 0) < k
    top = jnp.where(top_rows, pltpu.roll(carry, k, 0), rolled[:SUBLANES])
    return jnp.concatenate([top, rolled[SUBLANES:]], axis=0)


def _iota2(shape):
    return (lax.broadcasted_iota(jnp.int32, shape, 0), lax.broadcasted_iota(jnp.int32, shape, 1))


def _block_mm(width):
    n = width // INV_BLOCK
    row, col = _iota2((width, width))
    mask = (row // INV_BLOCK) == (col // INV_BLOCK)
    zero = jnp.zeros((), BF16)

    def mm(xs, ys):
        xb = [x.astype(BF16) for x in xs]
        yb = [y.astype(BF16) for y in ys]
        yb = [jnp.where(mask, jnp.concatenate([y] * n, axis=0), zero) for y in yb]
        return [_dot(x, y) for x, y in zip(xb, yb)]
    return mm


def _tri_inv(a_list, mm):
    t, width = a_list[0].shape
    assert t % INV_BLOCK == 0 and t <= 4 * INV_BLOCK and width % t == 0
    nb = t // INV_BLOCK
    row_b, col_b = _iota2((INV_BLOCK, width))
    eye_b = (row_b == col_b % INV_BLOCK).astype(F32)
    lane_blk = (col_b % t) // INV_BLOCK
    d = []
    for a in a_list:
        acc = None
        for b in range(nb):
            piece = jnp.where(lane_blk == b, a[b * INV_BLOCK:(b + 1) * INV_BLOCK, :], 0.0)
            acc = piece if acc is None else acc + piece
        d.append(acc)
    mmb = _block_mm(width)
    d2 = mmb(d, d)
    yield
    x = [eye_b - di for di in d]
    d4 = mmb(d2, d2)
    yield
    x = [xi + pi for xi, pi in zip(x, mmb(x, d2))]
    yield
    d8 = mmb(d4, d4)
    yield
    x = [xi + pi for xi, pi in zip(x, mmb(x, d4))]
    yield
    x = [xi + pi for xi, pi in zip(x, mmb(x, d8))]
    yield
    if nb == 1:
        return x
    x = [jnp.concatenate([jnp.where(lane_blk == b, xi, 0.0) for b in range(nb)], axis=0) for xi in x]
    row, col = _iota2((t, width))
    col = col % t
    eye = (row == col).astype(F32)
    blk = (row // INV_BLOCK) == (col // INV_BLOCK)
    m = mm(x, [jnp.where(blk, 0.0, a) for a in a_list])
    yield
    m2 = mm(m, m)
    yield
    n = [eye - mi for mi in m]
    n = [ni + pi for ni, pi in zip(n, mm(n, m2))]
    yield
    return mm(n, x)


def _pair_mm(t):
    row, col = _iota2((2 * t, 2 * t))
    mask = (row // t) == (col // t)
    zero = jnp.zeros((), BF16)

    def mm(xs, ys):
        xb = [x.astype(BF16) for x in xs]
        yb = [y.astype(BF16) for y in ys]
        yb = [jnp.where(mask, jnp.concatenate([y, y], axis=0), zero) for y in yb]
        return [_dot(x, y) for x, y in zip(xb, yb)]
    return mm


def _rwkv_mixer(t, bb, f_rows, w, fbuf, state, ybuf, rg, tick):
    base = SUBLANES
    hd = R_HEAD_DIM
    pw = R_PAIR * hd

    lane = lax.broadcasted_iota(jnp.int32, (t, pw), 1)
    first = lane < hd

    def segsum(*xs):
        outs = []
        for x in xs:
            blocks = []
            for p in range(R_PAIRS):
                xp = x[:, p * pw:(p + 1) * pw]
                s0 = jnp.sum(jnp.where(first, xp, 0.0), axis=-1, keepdims=True)
                s1 = jnp.sum(jnp.where(first, 0.0, xp), axis=-1, keepdims=True)
                blocks.append(jnp.where(first, s0, s1))
            outs.append(jnp.concatenate(blocks, axis=1))
        return outs

    row_t, col_t = _iota2((t, t))
    tri = (row_t >= col_t).astype(BF16)

    def prep(bi):
        f = f_rows[bi]
        fp = _shift_rows(f, fbuf[bi], 1)
        fbuf[bi, base - 1:base, :] = f[t - 1:t, :]
        fm = f + (fp - f) * w.mu[...]

        r = fm[:, 0:D_R]
        k = fm[:, D_R:2 * D_R]
        v = fm[:, 2 * D_R:3 * D_R]
        wa = fm[:, 3 * D_R:3 * D_R + LORA_W + LORA_A]
        gl = fm[:, 3 * D_R + LORA_W + LORA_A:]

        ld = -DECAY_SCALE * _sigmoid(w.w0[...] + _mm1(jnp.tanh(wa), w.wlo[...]))
        a = _sigmoid(w.a0[...] + _mm1(wa, w.alo[...]))
        g = _mm1(_sigmoid(gl), w.gup[...])
        kk = k * w.kk[...]
        keff = k * (1.0 + (a - 1.0) * w.ka[...])
        kk_ss, bonus_dot = segsum(kk * kk, r * keff * w.rk[...])
        kk = kk * lax.rsqrt(kk_ss + 1e-12)
        kka = kk * a

        cum = _mm_exact_lhs(tri, ld)
        cl = cum[t - 1:t, :]
        e_mc = jnp.exp(-cum)
        e_lc = jnp.exp(cl - cum)
        stack_t = jnp.concatenate([keff * e_lc, kka * e_lc, jnp.broadcast_to(cl, (SUBLANES, D_R))],
                                  axis=0).T
        tick()
        return dict(
            rt=(r * jnp.exp(cum)).astype(BF16), kpt=(kk * jnp.exp(cum - ld)).astype(BF16),
            bt=(kka * e_mc).astype(BF16), ktl=(keff * e_mc).astype(BF16), vb=v.astype(BF16),
            kb_t=stack_t[:, :2 * t].astype(BF16),
            pt_col=jnp.exp(stack_t[:, 2 * t:2 * t + 1]),
            gate=g, bonus=bonus_dot * v)

    pre = [prep(bi) for bi in range(bb)]

    row_c, col_c = _iota2((R_PAIR * t, pw))
    m_ch = (row_c // t) == (col_c // hd)
    row_s, col_s = _iota2((pw, pw))
    m_s = (row_s // hd) == (col_s // hd)
    row_g, col_g = _iota2((t, 2 * R_PAIR * t))
    col_g = col_g % t
    strict_g = row_g > col_g
    incl_g = row_g >= col_g
    zero_b = jnp.zeros((), BF16)
    mm_pair = _pair_mm(t)

    def bd(x, mask):
        return jnp.where(mask, jnp.concatenate([x] * R_PAIR, axis=0), zero_b)

    units = [(bi, p) for bi in range(bb) for p in range(R_PAIRS)]
    n_u = range(len(units))
    pair = lambda name, u: pre[u[0]][name][:, u[1] * pw:(u[1] + 1) * pw]
    v_bd = [bd(pair("vb", u), m_ch) for u in units]
    lhs = [jnp.concatenate([pair("kpt", u), pair("rt", u)], axis=0) for u in units]
    rhs = [jnp.concatenate([bd(pair("bt", u), m_ch), bd(pair("ktl", u), m_ch)], axis=0)
           for u in units]
    gm = [_dot(lhs[i], rhs[i], nt=True) for i in n_u]
    yield
    top = [jnp.where(strict_g, gm[i][:t], 0.0) for i in n_u]
    bot = [jnp.where(incl_g, gm[i][t:], 0.0).astype(BF16) for i in n_u]
    av = [_dot(top[i][:, R_PAIR * t:].astype(BF16), v_bd[i]) for i in n_u]
    yield
    tinv = yield from _tri_inv([top[i][:, :R_PAIR * t] for i in n_u], mm_pair)
    yield
    wu = [_dot(tinv[i].astype(BF16),
               jnp.concatenate([bd(pair("kpt", u), m_ch), bd(av[i].astype(BF16), m_ch)], axis=1))
          for i, u in enumerate(units)]
    yield
    s = [state[u[0], u[1]] for u in units]
    ws = [_dot(jnp.concatenate([wu[i][:, :pw].astype(BF16), lhs[i][t:]], axis=0), s[i].astype(BF16))
          for i in n_u]
    yield
    zb = [(ws[i][:t] + wu[i][:, pw:]).astype(BF16) for i in n_u]
    yz = [_dot(bot[i], jnp.concatenate([bd(-zb[i], m_ch), v_bd[i]], axis=0)) for i in n_u]
    upd = [_dot(pre[u[0]]["kb_t"][u[1] * pw:(u[1] + 1) * pw],
                jnp.concatenate([pair("vb", u), -zb[i]], axis=0)) for i, u in enumerate(units)]
    yield
    for i, (bi, p) in enumerate(units):
        state[bi, p] = s[i] * pre[bi]["pt_col"][p * pw:(p + 1) * pw] + jnp.where(m_s, upd[i], 0.0)
        ybuf[bi, :, p * pw:(p + 1) * pw] = ws[i][t:] + yz[i]

    inv_n = 1.0 / hd
    for bi in range(bb):
        y = ybuf[bi]
        dlt = y - segsum(y)[0] * inv_n
        var = segsum(dlt * dlt)[0] * inv_n
        yn = dlt * lax.rsqrt(var + GN_EPS) * w.lnw[...] + w.lnb[...]
        rg[bi * t:(bi + 1) * t, 0:D_R] = ((yn + pre[bi]["bonus"]) * pre[bi]["gate"]).astype(rg.dtype)


def _gdn_mixer(t, bb, qkv_rows, z_rows, ba_rows, w, cbuf, state, rg, tick):
    base = SUBLANES
    hist = CONV_W - 1
    hdim = G_HEAD_DIM

    row, col = _iota2((t, t))
    tri = (row >= col).astype(BF16)
    mm = _mm1

    units = [(bi, h) for bi in range(bb) for h in range(G_HEADS)]
    qh, kh, vh, bcol, gcol, grow, grow2, glast, kt = {}, {}, {}, {}, {}, {}, {}, {}, {}
    for bi in range(bb):
        qkv = qkv_rows[bi]
        carry = cbuf[bi]
        conv = qkv * w.cw[hist:hist + 1, :]
        for j in range(hist):
            conv = conv + _shift_rows(qkv, carry, hist - j) * w.cw[j:j + 1, :]
        cbuf[bi, base - hist:base, :] = qkv[t - hist:t, :]
        conv = conv * _sigmoid(conv)

        ba = ba_rows[bi]
        beta_t = _sigmoid(ba)
        glog_t = -jnp.exp(w.alog[...]) * _softplus(ba + w.dtb[...])
        g_t = _mm_exact_lhs(tri, glog_t)
        g_tt = jnp.concatenate([g_t, g_t], axis=0).T
        k_norm = []
        for h in range(G_HEADS):
            sl = slice(h * hdim, (h + 1) * hdim)
            q = conv[:, sl]
            qh[bi, h] = q * lax.rsqrt(jnp.sum(q * q, axis=-1, keepdims=True) + 1e-6) * (hdim ** -0.5)
            k = conv[:, D_G + h * hdim:D_G + (h + 1) * hdim]
            kh[bi, h] = k * lax.rsqrt(jnp.sum(k * k, axis=-1, keepdims=True) + 1e-6)
            k_norm.append(kh[bi, h])
            vh[bi, h] = conv[:, 2 * D_G + h * hdim:2 * D_G + (h + 1) * hdim]
            bcol[bi, h] = beta_t[:, h:h + 1]
            gcol[bi, h] = g_t[:, G_HEADS + h:G_HEADS + h + 1]
            grow2[bi, h] = g_tt[G_HEADS + h:G_HEADS + h + 1, :]
            grow[bi, h] = grow2[bi, h][:, :t]
            glast[bi, h] = gcol[bi, h][t - 1:t, :]
        kt_all = jnp.concatenate(k_norm, axis=1).T
        for h in range(G_HEADS):
            kt[bi, h] = kt_all[h * hdim:(h + 1) * hdim]
        tick()

    pairs = [(bi, pp) for bi in range(bb) for pp in range(G_HEADS // 2)]
    n_p = range(len(pairs))
    heads_of = lambda pr: ((pr[0], 2 * pr[1]), (pr[0], 2 * pr[1] + 1))
    row_k, col_k = _iota2((2 * t, 2 * hdim))
    m_k = (row_k // t) == (col_k // hdim)
    top_rows = row_k < t
    row_p, col_p = _iota2((t, 2 * t))
    first = col_p < t
    col_p = col_p % t
    incl_p = row_p >= col_p
    strict_p = row_p > col_p
    zero_b = jnp.zeros((), BF16)
    mm_pair = _pair_mm(t)

    def bd_cols(x0, x1):
        xs = jnp.concatenate([x0, x1], axis=0)
        return jnp.concatenate([jnp.where(top_rows[:, :x0.shape[1]], xs, zero_b),
                                jnp.where(top_rows[:, :x0.shape[1]], zero_b, xs)], axis=1)

    kb, eg = {}, {}
    for u_ in units:
        kb[u_] = kh[u_] * bcol[u_]
        eg[u_] = jnp.exp(gcol[u_])
    dec_incl, lhs, rhs = [], [], []
    for pr in pairs:
        h0, h1 = heads_of(pr)
        dg = jnp.where(first, gcol[h0] - grow2[h0], gcol[h1] - grow2[h1])
        dec_incl.append(jnp.where(incl_p, jnp.exp(jnp.where(incl_p, dg, 0.0)), 0.0))
        lhs.append(jnp.concatenate([jnp.concatenate([kb[h0], kb[h1]], axis=1),
                                    jnp.concatenate([qh[h0], qh[h1]], axis=1)], axis=0).astype(BF16))
        khp = jnp.concatenate([kh[h0], kh[h1]], axis=1).astype(BF16)
        rhs.append(jnp.where(m_k, jnp.concatenate([khp, khp], axis=0), zero_b))
    gm = [_dot(lhs[i], rhs[i], nt=True) for i in n_p]
    a_mat = [gm[i][:t] * jnp.where(strict_p, dec_incl[i], 0.0) for i in n_p]
    qk = [(gm[i][t:] * dec_incl[i]).astype(BF16) for i in n_p]
    yield
    tinv = yield from _tri_inv(a_mat, mm_pair)
    yield
    uw_p = []
    for i, pr in enumerate(pairs):
        x0, x1 = [jnp.concatenate([vh[u_] * bcol[u_], kb[u_] * eg[u_]], axis=1).astype(BF16)
                  for u_ in heads_of(pr)]
        uw_p.append(_dot(tinv[i].astype(BF16), bd_cols(x0, x1)))
    yield
    uw = [uw_p[i // 2][:, (i % 2) * 2 * hdim:(i % 2 + 1) * 2 * hdim] for i in range(len(units))]
    s = [state[u_[0], u_[1]] for u_ in units]
    wqs = [mm(jnp.concatenate([uw[i][:, hdim:], qh[u_] * eg[u_]], axis=0), s[i])
           for i, u_ in enumerate(units)]
    yield
    v_new = [uw[i][:, :hdim] - wqs[i][:t] for i in range(len(units))]
    qkv_p = [_dot(qk[i], bd_cols(v_new[2 * i].astype(BF16), v_new[2 * i + 1].astype(BF16))) for i in n_p]
    qkv_new = [qkv_p[i // 2][:, (i % 2) * hdim:(i % 2 + 1) * hdim] for i in range(len(units))]
    kdv = [mm(kt[u_] * jnp.exp(glast[u_] - grow[u_]), v_new[i]) for i, u_ in enumerate(units)]
    yield
    for i, (bi, h) in enumerate(units):
        sl = slice(h * hdim, (h + 1) * hdim)
        state[bi, h] = s[i] * jnp.exp(glast[bi, h]) + kdv[i]
        o = wqs[i][t:] + qkv_new[i]
        zh = z_rows[bi][:, sl]
        o = (o * lax.rsqrt(jnp.mean(o * o, axis=-1, keepdims=True) + RMS_EPS) * w.ng[...]
             * (zh * _sigmoid(zh)))
        rg[bi * t:(bi + 1) * t, D_R + h * hdim:D_R + (h + 1) * hdim] = o.astype(rg.dtype)


_TAIL_WEIGHTS = ("wo", "lnf", "wgate", "wup", "wdown", "lnp", "wpg", "wpp", "fng")


def _tail_kernel(x_ref, rg_ref, p_ref, *rest):
    w = _Refs(_TAIL_WEIGHTS, rest[:len(_TAIL_WEIGHTS)])
    y_ref = rest[len(_TAIL_WEIGHTS)]
    x = x_ref[...] + _dot(rg_ref[...], w.wo[...])
    hb = _rms(x, w.lnf[...]).astype(BF16)
    d_ff = w.wgate.shape[1]
    ffn = None
    for j in range(d_ff // FF_TILE):
        cs = slice(j * FF_TILE, (j + 1) * FF_TILE)
        gate = _dot(hb, w.wgate[:, cs])
        up = _dot(hb, w.wup[:, cs])
        act = (gate * _sigmoid(gate)) * up
        down = _dot(act.astype(BF16), w.wdown[cs, :])
        ffn = down if ffn is None else ffn + down
    x = x + ffn
    hb = _rms(x, w.lnp[...]).astype(BF16)
    x = x + _sigmoid(_dot(hb, w.wpg[...])) * _dot(p_ref[...].astype(BF16), w.wpp[...])
    y_ref[...] = _rms(x, w.fng[...])


def _tail_call(x2, rg2, p2, weights):
    n, d = x2.shape
    tm = min(TAIL_TILE, n)
    tok = lambda width: pl.BlockSpec((tm, width), lambda i: (i, 0))
    return pl.pallas_call(
        _tail_kernel,
        grid=(n // tm,),
        in_specs=[tok(d), tok(rg2.shape[1]), tok(p2.shape[1])] + [_resident(a) for a in weights],
        out_specs=tok(d),
        out_shape=jax.ShapeDtypeStruct((n, d), F32),
        compiler_params=pltpu.CompilerParams(dimension_semantics=("arbitrary",),
                                             vmem_limit_bytes=VMEM_LIMIT_BYTES),
        name="tail",
    )(x2, rg2, p2, *weights)


_WEIGHT_NAMES = ("lnm", "win",
                 "mu", "w0", "wlo", "alo", "a0", "gup", "kk", "ka", "rk", "lnw", "lnb",
                 "cw", "alog", "dtb", "ng")


class _Refs:
    def __init__(self, names, refs):
        for n, r in zip(names, refs):
            setattr(self, n, r)


def _layer_kernel(t, n_chunks, x_ref, shift_ref, wkv0_ref, conv0_ref, gdn0_ref, *rest):
    nw = len(_WEIGHT_NAMES)
    w = _Refs(_WEIGHT_NAMES, rest[:nw])
    rg_out, shift_out, wkv_out, conv_out, gdn_out = rest[nw:nw + 5]
    fbuf, cbuf, rstate, gstate, ybuf, rg = rest[nw + 5:]
    c_idx = pl.program_id(1)
    bb = x_ref.shape[0]
    base = SUBLANES
    hist = CONV_W - 1
    hd = R_HEAD_DIM

    @pl.when(c_idx == 0)
    def _():
        fbuf[...] = jnp.zeros(fbuf.shape, F32)
        cbuf[...] = jnp.zeros(cbuf.shape, F32)
        fbuf[:, base - 1:base, :] = shift_ref[...]
        cbuf[:, base - hist:base, :] = conv0_ref[...]
        gstate[...] = gdn0_ref[...]
        rstate[...] = jnp.zeros(rstate.shape, F32)
        for bi in range(bb):
            for h in range(R_HEADS):
                o = (h % R_PAIR) * hd
                rstate[bi, h // R_PAIR, o:o + hd, o:o + hd] = wkv0_ref[bi, h]

    rows = lambda ref: jnp.concatenate([ref[bi] for bi in range(bb)], axis=0)

    hb = _rms(rows(x_ref), w.lnm[...]).astype(BF16)
    split = lambda a: [a[bi * t:(bi + 1) * t] for bi in range(bb)]
    f_rows = split(_dot(hb, w.win[:, 0:R_PROJ]))
    proj = {}

    def gdn_inproj():
        off_z = R_PROJ + G_QKV
        off_b = off_z + D_G
        for name, c0, c1 in (("qkv", R_PROJ, off_z), ("z", off_z, off_b), ("ba", off_b, off_b + LANES)):
            proj[name] = split(_dot(hb, w.win[:, c0:c1]))
            yield

    late = gdn_inproj()

    def tick():
        next(late, None)

    rwkv = _rwkv_mixer(t, bb, f_rows, w, fbuf, rstate, ybuf, rg, tick)
    next(rwkv)
    for _ in late:
        pass
    qkv_rows = proj["qkv"]
    running = [_gdn_mixer(t, bb, qkv_rows, proj["z"], proj["ba"], w, cbuf, gstate, rg, tick), rwkv]
    while running:
        for mixer in list(running):
            if next(mixer, "done") == "done":
                running.remove(mixer)
    for bi in range(bb):
        rg_out[bi] = rg[bi * t:(bi + 1) * t, :]

    @pl.when(c_idx == n_chunks - 1)
    def _():
        for bi in range(bb):
            shift_out[bi] = f_rows[bi][t - 1:t, :]
            conv_out[bi] = qkv_rows[bi][t - hist:t, :]
            for h in range(R_HEADS):
                o = (h % R_PAIR) * hd
                wkv_out[bi, h] = rstate[bi, h // R_PAIR, o:o + hd, o:o + hd]
        gdn_out[...] = gstate[...]


def _resident(a):
    nd = a.ndim
    return pl.BlockSpec(a.shape, lambda *_: (0,) * nd, pipeline_mode=pl.Buffered(1))


def _layer_call(x, shift0, wkv0_kv, conv0, gdn0, weights):
    b, l, d = x.shape
    t = min(CHUNK, l)
    nc = l // t
    bb = BATCH_BLOCK if b % BATCH_BLOCK == 0 else 1
    cur = lambda i, c: (i, c, 0)
    per_stream = lambda a: pl.BlockSpec((bb,) + a.shape[1:], lambda i, c: (i,) + (0,) * (a.ndim - 1))
    state_shapes = (shift0.shape, wkv0_kv.shape, conv0.shape, gdn0.shape)
    return pl.pallas_call(
        functools.partial(_layer_kernel, t, nc),
        grid=(b // bb, nc),
        in_specs=[pl.BlockSpec((bb, t, d), cur),
                  per_stream(shift0), per_stream(wkv0_kv), per_stream(conv0), per_stream(gdn0)]
                 + [_resident(a) for a in weights],
        out_specs=[pl.BlockSpec((bb, t, D_R + D_G), cur)]
                  + [pl.BlockSpec((bb,) + s[1:], lambda i, c, n=len(s): (i,) + (0,) * (n - 1))
                     for s in state_shapes],
        out_shape=[jax.ShapeDtypeStruct((b, l, D_R + D_G), BF16)]
                  + [jax.ShapeDtypeStruct(s, F32) for s in state_shapes],
        scratch_shapes=[pltpu.VMEM((bb, SUBLANES, R_PROJ), F32),
                        pltpu.VMEM((bb, SUBLANES, G_QKV), F32),
                        pltpu.VMEM((bb, R_PAIRS, R_PAIR * R_HEAD_DIM, R_PAIR * R_HEAD_DIM), F32),
                        pltpu.VMEM((bb, G_HEADS, G_HEAD_DIM, G_HEAD_DIM), F32),
                        pltpu.VMEM((bb, t, D_R), F32),
                        pltpu.VMEM((bb * t, D_R + D_G), BF16)],
        compiler_params=pltpu.CompilerParams(dimension_semantics=("arbitrary", "arbitrary"),
                                             vmem_limit_bytes=VMEM_LIMIT_BYTES),
        name="mixers",
    )(x, shift0, wkv0_kv, conv0, gdn0, *weights)


def _pad_lanes(a, lanes, offset=0):
    out = jnp.zeros(a.shape[:-1] + (lanes,), a.dtype)
    return lax.dynamic_update_slice_in_dim(out, a, offset, axis=a.ndim - 1)


def _prepare_weights(wts, final_g):
    (ln_mix_g, w_in, mu_shift, w0, w_lora_up, a0, a_lora_up, g_lora_up, k_k, k_a, r_k, ln_x_w, ln_x_b,
     conv_w, a_log, dt_bias, gdn_norm_g, w_out, ln_ffn_g, w_gate, w_up, w_down, ln_ple_g,
     w_ple_gate, w_ple_proj) = wts
    row = lambda a: a.reshape(1, -1).astype(F32)
    d_in_padded = R_PROJ + G_QKV + D_G + LANES
    zeros_lora = jnp.zeros((LORA_W, D_R), BF16)
    named = dict(
        lnm=row(ln_mix_g), win=_pad_lanes(w_in.astype(BF16), d_in_padded),
        mu=row(mu_shift), w0=row(w0),
        wlo=jnp.concatenate([w_lora_up.astype(BF16), zeros_lora], axis=0),
        alo=jnp.concatenate([zeros_lora, a_lora_up.astype(BF16)], axis=0),
        a0=row(a0), gup=g_lora_up.astype(BF16), kk=row(k_k), ka=row(k_a), rk=row(r_k),
        lnw=row(ln_x_w), lnb=row(ln_x_b),
        cw=conv_w, alog=_pad_lanes(row(a_log), LANES, G_HEADS), dtb=_pad_lanes(row(dt_bias), LANES, G_HEADS),
        ng=row(gdn_norm_g),
        wo=w_out.astype(BF16), lnf=row(ln_ffn_g), wgate=w_gate.astype(BF16), wup=w_up.astype(BF16),
        wdown=w_down.astype(BF16), lnp=row(ln_ple_g), wpg=w_ple_gate.astype(BF16),
        wpp=w_ple_proj.astype(BF16), fng=row(final_g))
    return (tuple(named[n] for n in _WEIGHT_NAMES), tuple(named[n] for n in _TAIL_WEIGHTS))


def _layer(x, p, shift0, wkv0, conv0, gdn0, weights):
    b, l, d = x.shape
    mixer_w, tail_w = weights
    rg, shift1, wkv1_kv, conv1, gdn1 = _layer_call(x, shift0.reshape(b, 1, R_PROJ),
                                                   jnp.swapaxes(wkv0, -1, -2), conv0, gdn0, mixer_w)
    y = _tail_call(x.reshape(b * l, d), rg.reshape(b * l, D_R + D_G), p.reshape(b * l, -1), tail_w)
    return y.reshape(b, l, d), shift1.reshape(b, R_PROJ), jnp.swapaxes(wkv1_kv, -1, -2), conv1, gdn1


def kernel(x_prompt, x_sample, state_shift, state_wkv, state_conv, state_gdn, p_prompt, p_sample,
           ln_mix_g, w_in, mu_shift, w0, w_lora_up, a0, a_lora_up, g_lora_up, k_k, k_a, r_k,
           ln_x_w, ln_x_b, conv_w, a_log, dt_bias, gdn_norm_g, w_out, ln_ffn_g, w_gate, w_up,
           w_down, ln_ple_g, w_ple_gate, w_ple_proj, final_norm_g):
    depth = w_in.shape[0]
    assert depth == 1, "the final norm is fused into the single layer's kernel"
    bp = x_prompt.shape[0]
    dt = x_prompt.dtype
    wts = tuple(a[0] for a in (ln_mix_g, w_in, mu_shift, w0, w_lora_up, a0, a_lora_up, g_lora_up, k_k,
                               k_a, r_k, ln_x_w, ln_x_b, conv_w, a_log, dt_bias, gdn_norm_g, w_out,
                               ln_ffn_g, w_gate, w_up, w_down, ln_ple_g, w_ple_gate, w_ple_proj))
    weights = _prepare_weights(wts, final_norm_g)
    z_shift = jnp.zeros((bp, R_PROJ), dt)
    z_wkv = jnp.zeros((bp, R_HEADS, R_HEAD_DIM, R_HEAD_DIM), dt)
    z_conv = jnp.zeros((bp, CONV_W - 1, G_QKV), dt)
    z_gdn = jnp.zeros((bp, G_HEADS, G_HEAD_DIM, G_HEAD_DIM), dt)
    yp, a1, a2, a3, a4 = _layer(x_prompt, p_prompt[0], z_shift, z_wkv, z_conv, z_gdn, weights)
    ys, b1, b2, b3, b4 = _layer(x_sample, p_sample[0], state_shift[0], state_wkv[0], state_conv[0],
                                state_gdn[0], weights)
    stack = lambda a: a[None]
    return (yp, ys, stack(a1), stack(a2), stack(a3), stack(a4),
            stack(b1), stack(b2), stack(b3), stack(b4))
```

```python
import functools
import math

import jax
import jax.numpy as jnp
from jax import lax
from jax.experimental import pallas as pl
from jax.experimental.pallas import tpu as pltpu

F32 = jnp.float32
BF16 = jnp.bfloat16

R_HEADS = 8
R_HEAD_DIM = 64
D_R = R_HEADS * R_HEAD_DIM
LORA_W = 64
LORA_A = 64
LORA_G = 128
R_PROJ = 3 * D_R + LORA_W + LORA_A + LORA_G
G_HEADS = 4
G_HEAD_DIM = 128
D_G = G_HEADS * G_HEAD_DIM
CONV_W = 4
G_QKV = 3 * D_G
CHUNK = 64
RMS_EPS = 1e-6
GN_EPS = 64e-5
DECAY_SCALE = math.exp(-0.5)

LANES = 128
SUBLANES = 8
VMEM_LIMIT_BYTES = 60 * 1024 * 1024
INV_BLOCK = 16
FF_TILE = 256
BATCH_BLOCK = 4
TAIL_TILE = 1024
R_PAIR = LANES // R_HEAD_DIM
R_PAIRS = R_HEADS // R_PAIR


def _dot(a, b, nt=False):
    dn = (((1,), (1,)), ((), ())) if nt else (((1,), (0,)), ((), ()))
    return lax.dot_general(a, b, dn, preferred_element_type=F32)


def _mm1(a, b, nt=False):
    return _dot(a.astype(BF16), b.astype(BF16), nt)


def _mm_exact_lhs(a_bf16, b):
    b1 = b.astype(BF16)
    b2 = (b - b1.astype(F32)).astype(BF16)
    return _dot(a_bf16, b1) + _dot(a_bf16, b2)


def _sigmoid(x):
    return 0.5 * jnp.tanh(0.5 * x) + 0.5


def _softplus(x):
    return jnp.maximum(x, 0.0) + jnp.log1p(jnp.exp(-jnp.abs(x)))


def _rms(x, g):
    return x * lax.rsqrt(jnp.mean(x * x, axis=-1, keepdims=True) + RMS_EPS) * g


def _shift_rows(x, carry, k):
    rolled = pltpu.roll(x, k, 0)
    top_rows = lax.broadcasted_iota(jnp.int32, (SUBLANES, x.shape[1]), 0) < k
    top = jnp.where(top_rows, pltpu.roll(carry, k, 0), rolled[:SUBLANES])
    return jnp.concatenate([top, rolled[SUBLANES:]], axis=0)


def _iota2(shape):
    return (lax.broadcasted_iota(jnp.int32, shape, 0), lax.broadcasted_iota(jnp.int32, shape, 1))


def _block_mm(width):
    n = width // INV_BLOCK
    row, col = _iota2((width, width))
    mask = (row // INV_BLOCK) == (col // INV_BLOCK)
    zero = jnp.zeros((), BF16)

    def mm(xs, ys):
        xb = [x.astype(BF16) for x in xs]
        yb = [y.astype(BF16) for y in ys]
        yb = [jnp.where(mask, jnp.concatenate([y] * n, axis=0), zero) for y in yb]
        return [_dot(x, y) for x, y in zip(xb, yb)]
    return mm


def _tri_inv(a_list, mm):
    t, width = a_list[0].shape
    assert t % INV_BLOCK == 0 and t <= 4 * INV_BLOCK and width % t == 0
    nb = t // INV_BLOCK
    row_b, col_b = _iota2((INV_BLOCK, width))
    eye_b = (row_b == col_b % INV_BLOCK).astype(F32)
    lane_blk = (col_b % t) // INV_BLOCK
    d = []
    for a in a_list:
        acc = None
        for b in range(nb):
            piece = jnp.where(lane_blk == b, a[b * INV_BLOCK:(b + 1) * INV_BLOCK, :], 0.0)
            acc = piece if acc is None else acc + piece
        d.append(acc)
    mmb = _block_mm(width)
    d2 = mmb(d, d)
    yield
    x = [eye_b - di for di in d]
    d4 = mmb(d2, d2)
    yield
    x = [xi + pi for xi, pi in zip(x, mmb(x, d2))]
    yield
    d8 = mmb(d4, d4)
    yield
    x = [xi + pi for xi, pi in zip(x, mmb(x, d4))]
    yield
    x = [xi + pi for xi, pi in zip(x, mmb(x, d8))]
    yield
    if nb == 1:
        return x
    x = [jnp.concatenate([jnp.where(lane_blk == b, xi, 0.0) for b in range(nb)], axis=0) for xi in x]
    row, col = _iota2((t, width))
    col = col % t
    eye = (row == col).astype(F32)
    blk = (row // INV_BLOCK) == (col // INV_BLOCK)
    m = mm(x, [jnp.where(blk, 0.0, a) for a in a_list])
    yield
    m2 = mm(m, m)
    yield
    n = [eye - mi for mi in m]
    n = [ni + pi for ni, pi in zip(n, mm(n, m2))]
    yield
    return mm(n, x)


def _pair_mm(t):
    row, col = _iota2((2 * t, 2 * t))
    mask = (row // t) == (col // t)
    zero = jnp.zeros((), BF16)

    def mm(xs, ys):
        xb = [x.astype(BF16) for x in xs]
        yb = [y.astype(BF16) for y in ys]
        yb = [jnp.where(mask, jnp.concatenate([y, y], axis=0), zero) for y in yb]
        return [_dot(x, y) for x, y in zip(xb, yb)]
    return mm


def _rwkv_mixer(t, bb, f_rows, w, fbuf, state, ybuf, rg, tick):
    base = SUBLANES
    hd = R_HEAD_DIM
    pw = R_PAIR * hd

    lane = lax.broadcasted_iota(jnp.int32, (t, pw), 1)
    first = lane < hd

    def segsum(*xs):
        outs = []
        for x in xs:
            blocks = []
            for p in range(R_PAIRS):
                xp = x[:, p * pw:(p + 1) * pw]
                s0 = jnp.sum(jnp.where(first, xp, 0.0), axis=-1, keepdims=True)
                s1 = jnp.sum(jnp.where(first, 0.0, xp), axis=-1, keepdims=True)
                blocks.append(jnp.where(first, s0, s1))
            outs.append(jnp.concatenate(blocks, axis=1))
        return outs

    row_t, col_t = _iota2((t, t))
    tri = (row_t >= col_t).astype(BF16)

    def prep(bi):
        f = f_rows[bi]
        fp = _shift_rows(f, fbuf[bi], 1)
        fbuf[bi, base - 1:base, :] = f[t - 1:t, :]
        fm = f + (fp - f) * w.mu[...]

        r = fm[:, 0:D_R]
        k = fm[:, D_R:2 * D_R]
        v = fm[:, 2 * D_R:3 * D_R]
        wa = fm[:, 3 * D_R:3 * D_R + LORA_W + LORA_A]
        gl = fm[:, 3 * D_R + LORA_W + LORA_A:]

        ld = -DECAY_SCALE * _sigmoid(w.w0[...] + _mm1(jnp.tanh(wa), w.wlo[...]))
        a = _sigmoid(w.a0[...] + _mm1(wa, w.alo[...]))
        g = _mm1(_sigmoid(gl), w.gup[...])
        kk = k * w.kk[...]
        keff = k * (1.0 + (a - 1.0) * w.ka[...])
        kk_ss, bonus_dot = segsum(kk * kk, r * keff * w.rk[...])
        kk = kk * lax.rsqrt(kk_ss + 1e-12)
        kka = kk * a

        cum = _mm_exact_lhs(tri, ld)
        cl = cum[t - 1:t, :]
        e_mc = jnp.exp(-cum)
        e_lc = jnp.exp(cl - cum)
        stack_t = jnp.concatenate([keff * e_lc, kka * e_lc, jnp.broadcast_to(cl, (SUBLANES, D_R))],
                                  axis=0).T
        tick()
        return dict(
            rt=(r * jnp.exp(cum)).astype(BF16), kpt=(kk * jnp.exp(cum - ld)).astype(BF16),
            bt=(kka * e_mc).astype(BF16), ktl=(keff * e_mc).astype(BF16), vb=v.astype(BF16),
            kb_t=stack_t[:, :2 * t].astype(BF16),
            pt_col=jnp.exp(stack_t[:, 2 * t:2 * t + 1]),
            gate=g, bonus=bonus_dot * v)

    pre = [prep(bi) for bi in range(bb)]

    row_c, col_c = _iota2((R_PAIR * t, pw))
    m_ch = (row_c // t) == (col_c // hd)
    row_s, col_s = _iota2((pw, pw))
    m_s = (row_s // hd) == (col_s // hd)
    row_g, col_g = _iota2((t, 2 * R_PAIR * t))
    col_g = col_g % t
    strict_g = row_g > col_g
    incl_g = row_g >= col_g
    zero_b = jnp.zeros((), BF16)
    mm_pair = _pair_mm(t)

    def bd(x, mask):
        return jnp.where(mask, jnp.concatenate([x] * R_PAIR, axis=0), zero_b)

    units = [(bi, p) for bi in range(bb) for p in range(R_PAIRS)]
    n_u = range(len(units))
    pair = lambda name, u: pre[u[0]][name][:, u[1] * pw:(u[1] + 1) * pw]
    v_bd = [bd(pair("vb", u), m_ch) for u in units]
    lhs = [jnp.concatenate([pair("kpt", u), pair("rt", u)], axis=0) for u in units]
    rhs = [jnp.concatenate([bd(pair("bt", u), m_ch), bd(pair("ktl", u), m_ch)], axis=0)
           for u in units]
    gm = [_dot(lhs[i], rhs[i], nt=True) for i in n_u]
    yield
    top = [jnp.where(strict_g, gm[i][:t], 0.0) for i in n_u]
    bot = [jnp.where(incl_g, gm[i][t:], 0.0).astype(BF16) for i in n_u]
    av = [_dot(top[i][:, R_PAIR * t:].astype(BF16), v_bd[i]) for i in n_u]
    yield
    tinv = yield from _tri_inv([top[i][:, :R_PAIR * t] for i in n_u], mm_pair)
    yield
    wu = [_dot(tinv[i].astype(BF16),
               jnp.concatenate([bd(pair("kpt", u), m_ch), bd(av[i].astype(BF16), m_ch)], axis=1))
          for i, u in enumerate(units)]
    yield
    s = [state[u[0], u[1]] for u in units]
    ws = [_dot(jnp.concatenate([wu[i][:, :pw].astype(BF16), lhs[i][t:]], axis=0), s[i].astype(BF16))
          for i in n_u]
    yield
    zb = [(ws[i][:t] + wu[i][:, pw:]).astype(BF16) for i in n_u]
    yz = [_dot(bot[i], jnp.concatenate([bd(-zb[i], m_ch), v_bd[i]], axis=0)) for i in n_u]
    upd = [_dot(pre[u[0]]["kb_t"][u[1] * pw:(u[1] + 1) * pw],
                jnp.concatenate([pair("vb", u), -zb[i]], axis=0)) for i, u in enumerate(units)]
    yield
    for i, (bi, p) in enumerate(units):
        state[bi, p] = s[i] * pre[bi]["pt_col"][p * pw:(p + 1) * pw] + jnp.where(m_s, upd[i], 0.0)
        ybuf[bi, :, p * pw:(p + 1) * pw] = ws[i][t:] + yz[i]

    inv_n = 1.0 / hd
    for bi in range(bb):
        y = ybuf[bi]
        dlt = y - segsum(y)[0] * inv_n
        var = segsum(dlt * dlt)[0] * inv_n
        yn = dlt * lax.rsqrt(var + GN_EPS) * w.lnw[...] + w.lnb[...]
        rg[bi * t:(bi + 1) * t, 0:D_R] = ((yn + pre[bi]["bonus"]) * pre[bi]["gate"]).astype(rg.dtype)


def _gdn_mixer(t, bb, qkv_rows, z_rows, ba_rows, w, cbuf, state, rg, tick):
    base = SUBLANES
    hist = CONV_W - 1
    hdim = G_HEAD_DIM

    row, col = _iota2((t, t))
    tri = (row >= col).astype(BF16)
    mm = _mm1

    units = [(bi, h) for bi in range(bb) for h in range(G_HEADS)]
    qh, kh, vh, bcol, gcol, grow, grow2, glast, kt = {}, {}, {}, {}, {}, {}, {}, {}, {}
    for bi in range(bb):
        qkv = qkv_rows[bi]
        carry = cbuf[bi]
        conv = qkv * w.cw[hist:hist + 1, :]
        for j in range(hist):
            conv = conv + _shift_rows(qkv, carry, hist - j) * w.cw[j:j + 1, :]
        cbuf[bi, base - hist:base, :] = qkv[t - hist:t, :]
        conv = conv * _sigmoid(conv)

        ba = ba_rows[bi]
        beta_t = _sigmoid(ba)
        glog_t = -jnp.exp(w.alog[...]) * _softplus(ba + w.dtb[...])
        g_t = _mm_exact_lhs(tri, glog_t)
        g_tt = jnp.concatenate([g_t, g_t], axis=0).T
        k_norm = []
        for h in range(G_HEADS):
            sl = slice(h * hdim, (h + 1) * hdim)
            q = conv[:, sl]
            qh[bi, h] = q * lax.rsqrt(jnp.sum(q * q, axis=-1, keepdims=True) + 1e-6) * (hdim ** -0.5)
            k = conv[:, D_G + h * hdim:D_G + (h + 1) * hdim]
            kh[bi, h] = k * lax.rsqrt(jnp.sum(k * k, axis=-1, keepdims=True) + 1e-6)
            k_norm.append(kh[bi, h])
            vh[bi, h] = conv[:, 2 * D_G + h * hdim:2 * D_G + (h + 1) * hdim]
            bcol[bi, h] = beta_t[:, h:h + 1]
            gcol[bi, h] = g_t[:, G_HEADS + h:G_HEADS + h + 1]
            grow2[bi, h] = g_tt[G_HEADS + h:G_HEADS + h + 1, :]
            grow[bi, h] = grow2[bi, h][:, :t]
            glast[bi, h] = gcol[bi, h][t - 1:t, :]
        kt_all = jnp.concatenate(k_norm, axis=1).T
        for h in range(G_HEADS):
            kt[bi, h] = kt_all[h * hdim:(h + 1) * hdim]
        tick()

    pairs = [(bi, pp) for bi in range(bb) for pp in range(G_HEADS // 2)]
    n_p = range(len(pairs))
    heads_of = lambda pr: ((pr[0], 2 * pr[1]), (pr[0], 2 * pr[1] + 1))
    row_k, col_k = _iota2((2 * t, 2 * hdim))
    m_k = (row_k // t) == (col_k // hdim)
    top_rows = row_k < t
    row_p, col_p = _iota2((t, 2 * t))
    first = col_p < t
    col_p = col_p % t
    incl_p = row_p >= col_p
    strict_p = row_p > col_p
    zero_b = jnp.zeros((), BF16)
    mm_pair = _pair_mm(t)

    def bd_cols(x0, x1):
        xs = jnp.concatenate([x0, x1], axis=0)
        return jnp.concatenate([jnp.where(top_rows[:, :x0.shape[1]], xs, zero_b),
                                jnp.where(top_rows[:, :x0.shape[1]], zero_b, xs)], axis=1)

    kb, eg = {}, {}
    for u_ in units:
        kb[u_] = kh[u_] * bcol[u_]
        eg[u_] = jnp.exp(gcol[u_])
    dec_incl, lhs, rhs = [], [], []
    for pr in pairs:
        h0, h1 = heads_of(pr)
        dg = jnp.where(first, gcol[h0] - grow2[h0], gcol[h1] - grow2[h1])
        dec_incl.append(jnp.where(incl_p, jnp.exp(jnp.where(incl_p, dg, 0.0)), 0.0))
        lhs.append(jnp.concatenate([jnp.concatenate([kb[h0], kb[h1]], axis=1),
                                    jnp.concatenate([qh[h0], qh[h1]], axis=1)], axis=0).astype(BF16))
        khp = jnp.concatenate([kh[h0], kh[h1]], axis=1).astype(BF16)
        rhs.append(jnp.where(m_k, jnp.concatenate([khp, khp], axis=0), zero_b))
    gm = [_dot(lhs[i], rhs[i], nt=True) for i in n_p]
    a_mat = [gm[i][:t] * jnp.where(strict_p, dec_incl[i], 0.0) for i in n_p]
    qk = [(gm[i][t:] * dec_incl[i]).astype(BF16) for i in n_p]
    yield
    tinv = yield from _tri_inv(a_mat, mm_pair)
    yield
    uw_p = []
    for i, pr in enumerate(pairs):
        x0, x1 = [jnp.concatenate([vh[u_] * bcol[u_], kb[u_] * eg[u_]], axis=1).astype(BF16)
                  for u_ in heads_of(pr)]
        uw_p.append(_dot(tinv[i].astype(BF16), bd_cols(x0, x1)))
    yield
    uw = [uw_p[i // 2][:, (i % 2) * 2 * hdim:(i % 2 + 1) * 2 * hdim] for i in range(len(units))]
    s = [state[u_[0], u_[1]] for u_ in units]
    wqs = [mm(jnp.concatenate([uw[i][:, hdim:], qh[u_] * eg[u_]], axis=0), s[i])
           for i, u_ in enumerate(units)]
    yield
    v_new = [uw[i][:, :hdim] - wqs[i][:t] for i in range(len(units))]
    qkv_p = [_dot(qk[i], bd_cols(v_new[2 * i].astype(BF16), v_new[2 * i + 1].astype(BF16))) for i in n_p]
    qkv_new = [qkv_p[i // 2][:, (i % 2) * hdim:(i % 2 + 1) * hdim] for i in range(len(units))]
    kdv = [mm(kt[u_] * jnp.exp(glast[u_] - grow[u_]), v_new[i]) for i, u_ in enumerate(units)]
    yield
    for i, (bi, h) in enumerate(units):
        sl = slice(h * hdim, (h + 1) * hdim)
        state[bi, h] = s[i] * jnp.exp(glast[bi, h]) + kdv[i]
        o = wqs[i][t:] + qkv_new[i]
        zh = z_rows[bi][:, sl]
        o = (o * lax.rsqrt(jnp.mean(o * o, axis=-1, keepdims=True) + RMS_EPS) * w.ng[...]
             * (zh * _sigmoid(zh)))
        rg[bi * t:(bi + 1) * t, D_R + h * hdim:D_R + (h + 1) * hdim] = o.astype(rg.dtype)


_TAIL_WEIGHTS = ("wo", "lnf", "wgate", "wup", "wdown", "lnp", "wpg", "wpp", "fng")


def _tail_kernel(x_ref, rg_ref, p_ref, *rest):
    w = _Refs(_TAIL_WEIGHTS, rest[:len(_TAIL_WEIGHTS)])
    y_ref = rest[len(_TAIL_WEIGHTS)]
    x = x_ref[...] + _dot(rg_ref[...], w.wo[...])
    hb = _rms(x, w.lnf[...]).astype(BF16)
    d_ff = w.wgate.shape[1]
    ffn = None
    for j in range(d_ff // FF_TILE):
        cs = slice(j * FF_TILE, (j + 1) * FF_TILE)
        gate = _dot(hb, w.wgate[:, cs])
        up = _dot(hb, w.wup[:, cs])
        act = (gate * _sigmoid(gate)) * up
        down = _dot(act.astype(BF16), w.wdown[cs, :])
        ffn = down if ffn is None else ffn + down
    x = x + ffn
    hb = _rms(x, w.lnp[...]).astype(BF16)
    x = x + _sigmoid(_dot(hb, w.wpg[...])) * _dot(p_ref[...].astype(BF16), w.wpp[...])
    y_ref[...] = _rms(x, w.fng[...])


def _tail_call(x2, rg2, p2, weights):
    n, d = x2.shape
    tm = min(TAIL_TILE, n)
    tok = lambda width: pl.BlockSpec((tm, width), lambda i: (i, 0))
    return pl.pallas_call(
        _tail_kernel,
        grid=(n // tm,),
        in_specs=[tok(d), tok(rg2.shape[1]), tok(p2.shape[1])] + [_resident(a) for a in weights],
        out_specs=tok(d),
        out_shape=jax.ShapeDtypeStruct((n, d), F32),
        compiler_params=pltpu.CompilerParams(dimension_semantics=("arbitrary",),
                                             vmem_limit_bytes=VMEM_LIMIT_BYTES),
        name="tail",
    )(x2, rg2, p2, *weights)


_WEIGHT_NAMES = ("lnm", "win",
                 "mu", "w0", "wlo", "alo", "a0", "gup", "kk", "ka", "rk", "lnw", "lnb",
                 "cw", "alog", "dtb", "ng")


class _Refs:
    def __init__(self, names, refs):
        for n, r in zip(names, refs):
            setattr(self, n, r)


def _layer_kernel(t, n_chunks, x_ref, shift_ref, wkv0_ref, conv0_ref, gdn0_ref, *rest):
    nw = len(_WEIGHT_NAMES)
    w = _Refs(_WEIGHT_NAMES, rest[:nw])
    rg_out, shift_out, wkv_out, conv_out, gdn_out = rest[nw:nw + 5]
    fbuf, cbuf, rstate, gstate, ybuf, rg = rest[nw + 5:]
    c_idx = pl.program_id(1)
    bb = x_ref.shape[0]
    base = SUBLANES
    hist = CONV_W - 1
    hd = R_HEAD_DIM

    @pl.when(c_idx == 0)
    def _():
        fbuf[...] = jnp.zeros(fbuf.shape, F32)
        cbuf[...] = jnp.zeros(cbuf.shape, F32)
        fbuf[:, base - 1:base, :] = shift_ref[...]
        cbuf[:, base - hist:base, :] = conv0_ref[...]
        gstate[...] = gdn0_ref[...]
        rstate[...] = jnp.zeros(rstate.shape, F32)
        for bi in range(bb):
            for h in range(R_HEADS):
                o = (h % R_PAIR) * hd
                rstate[bi, h // R_PAIR, o:o + hd, o:o + hd] = wkv0_ref[bi, h]

    rows = lambda ref: jnp.concatenate([ref[bi] for bi in range(bb)], axis=0)

    hb = _rms(rows(x_ref), w.lnm[...]).astype(BF16)
    split = lambda a: [a[bi * t:(bi + 1) * t] for bi in range(bb)]
    f_rows = split(_dot(hb, w.win[:, 0:R_PROJ]))
    proj = {}

    def gdn_inproj():
        off_z = R_PROJ + G_QKV
        off_b = off_z + D_G
        for name, c0, c1 in (("qkv", R_PROJ, off_z), ("z", off_z, off_b), ("ba", off_b, off_b + LANES)):
            proj[name] = split(_dot(hb, w.win[:, c0:c1]))
            yield

    late = gdn_inproj()

    def tick():
        next(late, None)

    rwkv = _rwkv_mixer(t, bb, f_rows, w, fbuf, rstate, ybuf, rg, tick)
    next(rwkv)
    for _ in late:
        pass
    qkv_rows = proj["qkv"]
    running = [rwkv, _gdn_mixer(t, bb, qkv_rows, proj["z"], proj["ba"], w, cbuf, gstate, rg, tick)]
    while running:
        for mixer in list(running):
            if next(mixer, "done") == "done":
                running.remove(mixer)
    for bi in range(bb):
        rg_out[bi] = rg[bi * t:(bi + 1) * t, :]

    @pl.when(c_idx == n_chunks - 1)
    def _():
        for bi in range(bb):
            shift_out[bi] = f_rows[bi][t - 1:t, :]
            conv_out[bi] = qkv_rows[bi][t - hist:t, :]
            for h in range(R_HEADS):
                o = (h % R_PAIR) * hd
                wkv_out[bi, h] = rstate[bi, h // R_PAIR, o:o + hd, o:o + hd]
        gdn_out[...] = gstate[...]


def _resident(a):
    nd = a.ndim
    return pl.BlockSpec(a.shape, lambda *_: (0,) * nd, pipeline_mode=pl.Buffered(1))


def _layer_call(x, shift0, wkv0_kv, conv0, gdn0, weights):
    b, l, d = x.shape
    t = min(CHUNK, l)
    nc = l // t
    bb = BATCH_BLOCK if b % BATCH_BLOCK == 0 else 1
    cur = lambda i, c: (i, c, 0)
    per_stream = lambda a: pl.BlockSpec((bb,) + a.shape[1:], lambda i, c: (i,) + (0,) * (a.ndim - 1))
    state_shapes = (shift0.shape, wkv0_kv.shape, conv0.shape, gdn0.shape)
    return pl.pallas_call(
        functools.partial(_layer_kernel, t, nc),
        grid=(b // bb, nc),
        in_specs=[pl.BlockSpec((bb, t, d), cur),
                  per_stream(shift0), per_stream(wkv0_kv), per_stream(conv0), per_stream(gdn0)]
                 + [_resident(a) for a in weights],
        out_specs=[pl.BlockSpec((bb, t, D_R + D_G), cur)]
                  + [pl.BlockSpec((bb,) + s[1:], lambda i, c, n=len(s): (i,) + (0,) * (n - 1))
                     for s in state_shapes],
        out_shape=[jax.ShapeDtypeStruct((b, l, D_R + D_G), BF16)]
                  + [jax.ShapeDtypeStruct(s, F32) for s in state_shapes],
        scratch_shapes=[pltpu.VMEM((bb, SUBLANES, R_PROJ), F32),
                        pltpu.VMEM((bb, SUBLANES, G_QKV), F32),
                        pltpu.VMEM((bb, R_PAIRS, R_PAIR * R_HEAD_DIM, R_PAIR * R_HEAD_DIM), F32),
                        pltpu.VMEM((bb, G_HEADS, G_HEAD_DIM, G_HEAD_DIM), F32),
                        pltpu.VMEM((bb, t, D_R), F32),
                        pltpu.VMEM((bb * t, D_R + D_G), BF16)],
        compiler_params=pltpu.CompilerParams(dimension_semantics=("arbitrary", "arbitrary"),
                                             vmem_limit_bytes=VMEM_LIMIT_BYTES),
        name="mixers",
    )(x, shift0, wkv0_kv, conv0, gdn0, *weights)


def _pad_lanes(a, lanes, offset=0):
    out = jnp.zeros(a.shape[:-1] + (lanes,), a.dtype)
    return lax.dynamic_update_slice_in_dim(out, a, offset, axis=a.ndim - 1)


def _prepare_weights(wts, final_g):
    (ln_mix_g, w_in, mu_shift, w0, w_lora_up, a0, a_lora_up, g_lora_up, k_k, k_a, r_k, ln_x_w, ln_x_b,
     conv_w, a_log, dt_bias, gdn_norm_g, w_out, ln_ffn_g, w_gate, w_up, w_down, ln_ple_g,
     w_ple_gate, w_ple_proj) = wts
    row = lambda a: a.reshape(1, -1).astype(F32)
    d_in_padded = R_PROJ + G_QKV + D_G + LANES
    zeros_lora = jnp.zeros((LORA_W, D_R), BF16)
    named = dict(
        lnm=row(ln_mix_g), win=_pad_lanes(w_in.astype(BF16), d_in_padded),
        mu=row(mu_shift), w0=row(w0),
        wlo=jnp.concatenate([w_lora_up.astype(BF16), zeros_lora], axis=0),
        alo=jnp.concatenate([zeros_lora, a_lora_up.astype(BF16)], axis=0),
        a0=row(a0), gup=g_lora_up.astype(BF16), kk=row(k_k), ka=row(k_a), rk=row(r_k),
        lnw=row(ln_x_w), lnb=row(ln_x_b),
        cw=conv_w, alog=_pad_lanes(row(a_log), LANES, G_HEADS), dtb=_pad_lanes(row(dt_bias), LANES, G_HEADS),
        ng=row(gdn_norm_g),
        wo=w_out.astype(BF16), lnf=row(ln_ffn_g), wgate=w_gate.astype(BF16), wup=w_up.astype(BF16),
        wdown=w_down.astype(BF16), lnp=row(ln_ple_g), wpg=w_ple_gate.astype(BF16),
        wpp=w_ple_proj.astype(BF16), fng=row(final_g))
    return (tuple(named[n] for n in _WEIGHT_NAMES), tuple(named[n] for n in _TAIL_WEIGHTS))


def _layer(x, p, shift0, wkv0, conv0, gdn0, weights):
    b, l, d = x.shape
    mixer_w, tail_w = weights
    rg, shift1, wkv1_kv, conv1, gdn1 = _layer_call(x, shift0.reshape(b, 1, R_PROJ),
                                                   jnp.swapaxes(wkv0, -1, -2), conv0, gdn0, mixer_w)
    y = _tail_call(x.reshape(b * l, d), rg.reshape(b * l, D_R + D_G), p.reshape(b * l, -1), tail_w)
    return y.reshape(b, l, d), shift1.reshape(b, R_PROJ), jnp.swapaxes(wkv1_kv, -1, -2), conv1, gdn1


def kernel(x_prompt, x_sample, state_shift, state_wkv, state_conv, state_gdn, p_prompt, p_sample,
           ln_mix_g, w_in, mu_shift, w0, w_lora_up, a0, a_lora_up, g_lora_up, k_k, k_a, r_k,
           ln_x_w, ln_x_b, conv_w, a_log, dt_bias, gdn_norm_g, w_out, ln_ffn_g, w_gate, w_up,
           w_down, ln_ple_g, w_ple_gate, w_ple_proj, final_norm_g):
    depth = w_in.shape[0]
    assert depth == 1, "the final norm is fused into the single layer's kernel"
    bp = x_prompt.shape[0]
    dt = x_prompt.dtype
    wts = tuple(a[0] for a in (ln_mix_g, w_in, mu_shift, w0, w_lora_up, a0, a_lora_up, g_lora_up, k_k,
                               k_a, r_k, ln_x_w, ln_x_b, conv_w, a_log, dt_bias, gdn_norm_g, w_out,
                               ln_ffn_g, w_gate, w_up, w_down, ln_ple_g, w_ple_gate, w_ple_proj))
    weights = _prepare_weights(wts, final_norm_g)
    z_shift = jnp.zeros((bp, R_PROJ), dt)
    z_wkv = jnp.zeros((bp, R_HEADS, R_HEAD_DIM, R_HEAD_DIM), dt)
    z_conv = jnp.zeros((bp, CONV_W - 1, G_QKV), dt)
    z_gdn = jnp.zeros((bp, G_HEADS, G_HEAD_DIM, G_HEAD_DIM), dt)
    yp, a1, a2, a3, a4 = _layer(x_prompt, p_prompt[0], z_shift, z_wkv, z_conv, z_gdn, weights)
    ys, b1, b2, b3, b4 = _layer(x_sample, p_sample[0], state_shift[0], state_wkv[0], state_conv[0],
                                state_gdn[0], weights)
    stack = lambda a: a[None]
    return (yp, ys, stack(a1), stack(a2), stack(a3), stack(a4),
            stack(b1), stack(b2), stack(b3), stack(b4))
```
